```python
import math
import jax, jax.numpy as jnp
from jax import lax
import numpy as np

D_MODEL = 1024
BATCH = 32
SEQ = 2048
DEPTH = 4

N_MIXERS = 2
N_S5_LAYERS = (DEPTH + 1) // 2
N_POOL_LAYERS = DEPTH // 2
S5_GROUP_CH = 16
S5_GROUPS = D_MODEL // S5_GROUP_CH
S5_STATE = 64
S5_CHUNK = 128
DT_MIN = 1e-3
DT_MAX = 1e-1
POOL_WINDOWS = (2, 4, 8, 16)
POOL_GROUPS = len(POOL_WINDOWS)
POOL_GROUP_CH = D_MODEL // POOL_GROUPS
D_FF = 2816
CONV_WIDTH = 3
RMS_EPS = 1e-6

kernel_name = "hybrid_s5_pool_convffn"


def rms_norm(x, gain):
    xf = x.astype(jnp.float32)
    y = xf * lax.rsqrt(jnp.mean(xf * xf, axis=-1, keepdims=True) + RMS_EPS)
    return (y * gain.astype(jnp.float32)).astype(x.dtype)


def _complex_affine_combine(e1, e2):
    a1r, a1i, b1r, b1i = e1
    a2r, a2i, b2r, b2i = e2
    ar = a2r * a1r - a2i * a1i
    ai = a2r * a1i + a2i * a1r
    br = a2r * b1r - a2i * b1i + b2r
    bi = a2r * b1i + a2i * b1r + b2i
    return ar, ai, br, bi


def s5_mixer(u, lam_re, lam_im, log_dt, b_re, b_im, c_re, c_im, d_skip, w_glu, b_glu):
    f32 = jnp.float32
    bsz, l, d = u.shape
    lr, li = lam_re.astype(f32), lam_im.astype(f32)
    dt = jnp.exp(log_dt.astype(f32))[:, None]
    mag = jnp.exp(lr * dt)
    ab_re = mag * jnp.cos(li * dt)
    ab_im = mag * jnp.sin(li * dt)
    den = lr * lr + li * li
    nr = ab_re - 1.0
    ni = ab_im
    f_re = ((nr * lr + ni * li) / den)[..., None]
    f_im = ((ni * lr - nr * li) / den)[..., None]
    br, bi = b_re.astype(f32), b_im.astype(f32)
    bb_re = f_re * br - f_im * bi
    bb_im = f_re * bi + f_im * br
    cr, ci = c_re.astype(f32), c_im.astype(f32)

    n_chunks = l // S5_CHUNK
    uf = u.astype(f32)
    u_chunks = uf.reshape(bsz, n_chunks, S5_CHUNK, S5_GROUPS, S5_GROUP_CH).transpose(1, 0, 2, 3, 4)
    a_re = jnp.broadcast_to(ab_re, (1, S5_CHUNK, S5_GROUPS, S5_STATE))
    a_im = jnp.broadcast_to(ab_im, (1, S5_CHUNK, S5_GROUPS, S5_STATE))

    def step(carry, u_c):
        h0r, h0i = carry
        xr = jnp.einsum('bcgh,gph->bcgp', u_c, bb_re)
        xi = jnp.einsum('bcgh,gph->bcgp', u_c, bb_im)
        pr, pi_, sr, si = lax.associative_scan(_complex_affine_combine, (a_re, a_im, xr, xi), axis=1)
        h0r_, h0i_ = h0r[:, None], h0i[:, None]
        hr = sr + pr * h0r_ - pi_ * h0i_
        hi = si + pr * h0i_ + pi_ * h0r_
        y = jnp.einsum('bcgp,ghp->bcgh', hr, cr) - jnp.einsum('bcgp,ghp->bcgh', hi, ci)
        return (hr[:, -1], hi[:, -1]), y

    init = (jnp.zeros((bsz, S5_GROUPS, S5_STATE), f32), jnp.zeros((bsz, S5_GROUPS, S5_STATE), f32))
    _, ys = lax.scan(step, init, u_chunks)
    y = ys.transpose(1, 0, 2, 3, 4).reshape(bsz, l, d) + d_skip.astype(f32) * uf
    y = jax.nn.gelu(y)
    out = y * jax.nn.sigmoid(y @ w_glu.astype(f32) + b_glu.astype(f32))
    return out.astype(u.dtype)


def pool_mixer(u, w_group, scale):
    f32 = jnp.float32
    bsz, l, d = u.shape
    uf = u.astype(f32)
    cs = jnp.concatenate([jnp.zeros((bsz, 1, d), f32), jnp.cumsum(uf, axis=1)], axis=1)
    t = jnp.arange(1, l + 1, dtype=f32)
    pooled = []
    for g, win in enumerate(POOL_WINDOWS):
        csg = cs[:, :, g * POOL_GROUP_CH:(g + 1) * POOL_GROUP_CH]
        upper = csg[:, 1:]
        lower = jnp.pad(csg, ((0, 0), (win - 1, 0), (0, 0)))[:, :l]
        count = jnp.minimum(t, float(win))[None, :, None]
        pooled.append((upper - lower) / count)
    pooled = jnp.stack(pooled, axis=2)
    diff = pooled - uf.reshape(bsz, l, POOL_GROUPS, POOL_GROUP_CH)
    out = jnp.einsum('bsgc,gcd->bsgd', diff, w_group.astype(f32)).reshape(bsz, l, d)
    return (out * scale.astype(f32)).astype(u.dtype)


def conv_ffn(u, w_gate, w_val, conv_w, conv_b, w_down):
    l = u.shape[1]
    g = u @ w_gate
    gp = jnp.pad(g, ((0, 0), (CONV_WIDTH - 1, 0), (0, 0)))
    gc = conv_b + conv_w[0] * gp[:, 0:l]
    for k in range(1, CONV_WIDTH):
        gc = gc + conv_w[k] * gp[:, k:k + l]
    hdn = jax.nn.gelu(gc) * (u @ w_val)
    return hdn @ w_down


def setup_inputs(seed: int = 0) -> dict:
    key = jax.random.key(seed)
    ks = jax.random.split(key, 22)
    f32 = jnp.float32
    na, nb = N_S5_LAYERS, N_POOL_LAYERS
    G, P, H, D, F = S5_GROUPS, S5_STATE, S5_GROUP_CH, D_MODEL, D_FF
    nrm = lambda k, s: jax.random.normal(k, s, f32)
    n_idx = jnp.arange(P, dtype=f32)
    return {
        "x": nrm(ks[0], (BATCH, SEQ, D)),
        "s5_lambda_re": -0.5 + 0.01 * nrm(ks[1], (na, G, P)),
        "s5_lambda_im": math.pi * n_idx + 0.01 * nrm(ks[2], (na, G, P)),
        "s5_log_dt": jax.random.uniform(ks[3], (na, G), f32, math.log(DT_MIN), math.log(DT_MAX)),
        "s5_b_re": nrm(ks[4], (na, G, P, H)) * (2 * H) ** -0.5,
        "s5_b_im": nrm(ks[5], (na, G, P, H)) * (2 * H) ** -0.5,
        "s5_c_re": nrm(ks[6], (na, G, H, P)) * P ** -0.5,
        "s5_c_im": nrm(ks[7], (na, G, H, P)) * P ** -0.5,
        "s5_d": nrm(ks[8], (na, D)),
        "s5_w_glu": nrm(ks[9], (na, D, D)) * D ** -0.5,
        "s5_b_glu": 0.01 * nrm(ks[10], (na, D)),
        "pool_w": nrm(ks[11], (nb, POOL_GROUPS, POOL_GROUP_CH, POOL_GROUP_CH)) * POOL_GROUP_CH ** -0.5,
        "pool_scale": 1.0 + 0.1 * nrm(ks[12], (nb, D)),
        "ffn_w_gate": nrm(ks[13], (DEPTH, D, F)) * D ** -0.5,
        "ffn_w_val": nrm(ks[14], (DEPTH, D, F)) * D ** -0.5,
        "ffn_conv_w": nrm(ks[15], (DEPTH, CONV_WIDTH, F)) * CONV_WIDTH ** -0.5,
        "ffn_conv_b": 0.01 * nrm(ks[16], (DEPTH, F)),
        "ffn_w_down": nrm(ks[17], (DEPTH, F, D)) * F ** -0.5,
        "norm_mix_pre": 1.0 + 0.05 * nrm(ks[18], (DEPTH, D)),
        "norm_mix_post": 1.0 + 0.05 * nrm(ks[19], (DEPTH, D)),
        "norm_ffn_pre": 1.0 + 0.05 * nrm(ks[20], (DEPTH, D)),
        "norm_ffn_post": 1.0 + 0.05 * nrm(ks[21], (DEPTH, D)),
    }


def reference(x, s5_lambda_re, s5_lambda_im, s5_log_dt, s5_b_re, s5_b_im, s5_c_re, s5_c_im,
              s5_d, s5_w_glu, s5_b_glu, pool_w, pool_scale, ffn_w_gate, ffn_w_val, ffn_conv_w,
              ffn_conv_b, ffn_w_down, norm_mix_pre, norm_mix_post, norm_ffn_pre, norm_ffn_post):
    for i in range(DEPTH):
        j = i // N_MIXERS
        h = rms_norm(x, norm_mix_pre[i])
        if i % N_MIXERS == 0:
            m = s5_mixer(h, s5_lambda_re[j], s5_lambda_im[j], s5_log_dt[j], s5_b_re[j], s5_b_im[j],
                         s5_c_re[j], s5_c_im[j], s5_d[j], s5_w_glu[j], s5_b_glu[j])
        else:
            m = pool_mixer(h, pool_w[j], pool_scale[j])
        x = x + rms_norm(m, norm_mix_post[i])
        h = rms_norm(x, norm_ffn_pre[i])
        f = conv_ffn(h, ffn_w_gate[i], ffn_w_val[i], ffn_conv_w[i], ffn_conv_b[i], ffn_w_down[i])
        x = x + rms_norm(f, norm_ffn_post[i])
    return x
```

```python
import functools
import math

import jax
import jax.numpy as jnp
from jax import lax
from jax.experimental import pallas as pl
from jax.experimental.pallas import tpu as pltpu

D_MODEL = 1024
BATCH = 32
SEQ = 2048
DEPTH = 4
S5_GROUP_CH = 16
S5_GROUPS = D_MODEL // S5_GROUP_CH
S5_STATE = 64
POOL_WINDOWS = (2, 4, 8, 16)
POOL_GROUP_CH = D_MODEL // len(POOL_WINDOWS)
D_FF = 2816
CONV_WIDTH = 3
RMS_EPS = 1e-6

SUBLANES = 8
LANES = 128

N_BLK = BATCH // SUBLANES
ROWS = SEQ * SUBLANES
S5_JBLK = 8
S5_NJ = S5_GROUPS // S5_JBLK
S5_JCH = S5_JBLK * S5_GROUP_CH
S5_JST = S5_JBLK * S5_STATE
FF_CHUNK = 256
FF_NCHUNK = D_FF // FF_CHUNK
POOL_HALO = (max(POOL_WINDOWS) - 1) * SUBLANES + SUBLANES

TM_S5 = 512
TM_POOL = 1024
TM_FFN = 512
VMEM_LIMIT = 56 * 1024 * 1024

F32 = jnp.float32
BF16 = jnp.bfloat16


def _rms(x, gain):
    return x * lax.rsqrt(jnp.mean(x * x, axis=-1, keepdims=True) + RMS_EPS) * gain


def _s5_prep_kernel(lr_ref, li_ref, ldt_ref, br_ref, bi_ref, are_ref, aim_ref, bbr_ref, bbi_ref):
    lr = lr_ref[...]
    li = li_ref[...]
    dt = jnp.exp(ldt_ref[...])
    mag = jnp.exp(lr * dt)
    ab_re = mag * jnp.cos(li * dt)
    ab_im = mag * jnp.sin(li * dt)
    den = lr * lr + li * li
    nr = ab_re - 1.0
    ni = ab_im
    f_re = (nr * lr + ni * li) / den
    f_im = (ni * lr - nr * li) / den
    are_ref[...] = ab_re
    aim_ref[...] = ab_im
    br = br_ref[...]
    bi = bi_ref[...]
    fr = f_re[:, None, :]
    fi = f_im[:, None, :]
    bbr_ref[...] = fr * br - fi * bi
    bbi_ref[...] = fr * bi + fi * br


def _s5_prep(lam_re, lam_im, log_dt, b_re, b_im):
    G, P, H = S5_GROUPS, S5_STATE, S5_GROUP_CH
    br_t = jnp.transpose(b_re, (0, 2, 1))
    bi_t = jnp.transpose(b_im, (0, 2, 1))
    return pl.pallas_call(
        _s5_prep_kernel,
        out_shape=(
            jax.ShapeDtypeStruct((G, P), F32),
            jax.ShapeDtypeStruct((G, P), F32),
            jax.ShapeDtypeStruct((G, H, P), F32),
            jax.ShapeDtypeStruct((G, H, P), F32),
        ),
        name="s5_prep",
    )(lam_re, lam_im, log_dt.reshape(G, 1), br_t, bi_t)


def _block_diag(m):
    nj, jb, r, c = m.shape
    eye = jnp.eye(jb, dtype=m.dtype)
    out = m[:, :, :, None, :] * eye[None, :, None, :, None]
    return out.reshape(nj, jb * r, jb * c)


def _s5_kernel(x_ref, gpre_ref, bmat_ref, are_ref, aim_ref, cr_ref, ci_ref, dskip_ref,
               wglu_ref, bglu_ref, gpost_ref, o_ref,
               ub_ref, xs_ref, y_ref, hst_ref, *, tm):
    j = pl.program_id(1)

    @pl.when(j == 0)
    def _():
        hst_ref[...] = jnp.zeros_like(hst_ref)

    x = x_ref[...]
    u = _rms(x, gpre_ref[...])
    ub_ref[...] = u.astype(BF16)
    nsteps = tm // SUBLANES

    for jb in range(S5_NJ):
        xs_ref[...] = jnp.dot(ub_ref[:, jb * S5_JCH:(jb + 1) * S5_JCH], bmat_ref[jb],
                              preferred_element_type=F32)
        ar = jnp.broadcast_to(are_ref[jb], (SUBLANES, S5_JST))
        ai = jnp.broadcast_to(aim_ref[jb], (SUBLANES, S5_JST))

        def step(t, carry):
            hr, hi = carry
            r0 = pl.multiple_of(t * SUBLANES, SUBLANES)
            xr = xs_ref[pl.ds(r0, SUBLANES), 0:S5_JST]
            xi = xs_ref[pl.ds(r0, SUBLANES), S5_JST:2 * S5_JST]
            nhr = ar * hr - ai * hi + xr
            nhi = ar * hi + ai * hr + xi
            xs_ref[pl.ds(r0, SUBLANES), 0:S5_JST] = nhr
            xs_ref[pl.ds(r0, SUBLANES), S5_JST:2 * S5_JST] = nhi
            return nhr, nhi

        h0 = (hst_ref[jb, :, 0:S5_JST], hst_ref[jb, :, S5_JST:2 * S5_JST])
        hr, hi = lax.fori_loop(0, nsteps, step, h0, unroll=8)
        hst_ref[jb, :, 0:S5_JST] = hr
        hst_ref[jb, :, S5_JST:2 * S5_JST] = hi

        yr = jnp.dot(xs_ref[:, 0:S5_JST].astype(BF16), cr_ref[jb], preferred_element_type=F32)
        yi = jnp.dot(xs_ref[:, S5_JST:2 * S5_JST].astype(BF16), ci_ref[jb], preferred_element_type=F32)
        y_ref[:, jb * S5_JCH:(jb + 1) * S5_JCH] = yr - yi

    y = y_ref[...] + dskip_ref[...] * u
    y = jax.nn.gelu(y)
    z = jnp.dot(y.astype(BF16), wglu_ref[...], preferred_element_type=F32) + bglu_ref[...]
    out = y * jax.nn.sigmoid(z)
    o_ref[...] = x + _rms(out, gpost_ref[...])


def _const_spec(shape):
    nd = len(shape)
    return pl.BlockSpec(shape, lambda i, j: (0,) * nd)


def _row_spec(tm):
    return pl.BlockSpec((None, tm, D_MODEL), lambda i, j: (i, j, 0))


def _s5_layer(xp, gpre, bmat, a_re, a_im, cr, ci, dskip, wglu, bglu, gpost):
    tm = TM_S5
    kern = functools.partial(_s5_kernel, tm=tm)
    return pl.pallas_call(
        kern,
        grid=(N_BLK, ROWS // tm),
        in_specs=[
            _row_spec(tm),
            _const_spec((1, D_MODEL)),
            _const_spec(bmat.shape),
            _const_spec(a_re.shape),
            _const_spec(a_im.shape),
            _const_spec(cr.shape),
            _const_spec(ci.shape),
            _const_spec((1, D_MODEL)),
            _const_spec(wglu.shape),
            _const_spec((1, D_MODEL)),
            _const_spec((1, D_MODEL)),
        ],
        out_specs=_row_spec(tm),
        out_shape=jax.ShapeDtypeStruct(xp.shape, F32),
        scratch_shapes=[
            pltpu.VMEM((tm, D_MODEL), BF16),
            pltpu.VMEM((tm, 2 * S5_JST), F32),
            pltpu.VMEM((tm, D_MODEL), F32),
            pltpu.VMEM((S5_NJ, SUBLANES, 2 * S5_JST), F32),
        ],
        compiler_params=pltpu.CompilerParams(
            dimension_semantics=("arbitrary", "arbitrary"),
            vmem_limit_bytes=VMEM_LIMIT),
        name="s5_mixer",
    )(xp, gpre, bmat, a_re, a_im, cr, ci, dskip, wglu, bglu, gpost)


def _pool_kernel(x_ref, gpre_ref, w_ref, scale_ref, gpost_ref, o_ref, buf_ref, m_ref, *, tm):
    j = pl.program_id(1)

    @pl.when(j == 0)
    def _():
        buf_ref[0:POOL_HALO, :] = jnp.zeros((POOL_HALO, D_MODEL), F32)

    @pl.when(j != 0)
    def _():
        buf_ref[0:POOL_HALO, :] = buf_ref[tm:tm + POOL_HALO, :]

    x = x_ref[...]
    u = _rms(x, gpre_ref[...])
    buf_ref[POOL_HALO:POOL_HALO + tm, :] = u

    row = lax.broadcasted_iota(jnp.int32, (tm, 1), 0)
    t1 = (j * (tm // SUBLANES) + row // SUBLANES + 1).astype(F32)

    for g, win in enumerate(POOL_WINDOWS):
        lo, hi = g * POOL_GROUP_CH, (g + 1) * POOL_GROUP_CH
        s = buf_ref[:, lo:hi]
        step = 1
        while step < win:
            sh = step * SUBLANES
            s = s[sh:, :] + s[:-sh, :]
            step *= 2
        s = s[s.shape[0] - tm:, :]
        cnt = jnp.minimum(t1, float(win))
        diff = s / cnt - u[:, lo:hi]
        m_ref[:, lo:hi] = jnp.dot(diff.astype(BF16), w_ref[g], preferred_element_type=F32)

    m = m_ref[...] * scale_ref[...]
    o_ref[...] = x + _rms(m, gpost_ref[...])


def _pool_layer(xp, gpre, w, scale, gpost):
    tm = TM_POOL
    kern = functools.partial(_pool_kernel, tm=tm)
    return pl.pallas_call(
        kern,
        grid=(N_BLK, ROWS // tm),
        in_specs=[
            _row_spec(tm),
            _const_spec((1, D_MODEL)),
            _const_spec(w.shape),
            _const_spec((1, D_MODEL)),
            _const_spec((1, D_MODEL)),
        ],
        out_specs=_row_spec(tm),
        out_shape=jax.ShapeDtypeStruct(xp.shape, F32),
        scratch_shapes=[
            pltpu.VMEM((POOL_HALO + tm, D_MODEL), F32),
            pltpu.VMEM((tm, D_MODEL), F32),
        ],
        compiler_params=pltpu.CompilerParams(
            dimension_semantics=("arbitrary", "arbitrary"),
            vmem_limit_bytes=VMEM_LIMIT),
        name="pool_mixer",
    )(xp, gpre, w, scale, gpost)


def _ffn_kernel(x_ref, gpre_ref, wg_ref, wv_ref, cw_ref, cb_ref, wd_ref, gpost_ref, o_ref,
                hb_ref, carry_ref, acc_ref, *, tm):
    j = pl.program_id(1)
    halo = (CONV_WIDTH - 1) * SUBLANES

    @pl.when(j == 0)
    def _():
        carry_ref[...] = jnp.zeros_like(carry_ref)

    x = x_ref[...]
    hb_ref[...] = _rms(x, gpre_ref[...]).astype(BF16)
    acc_ref[...] = jnp.zeros_like(acc_ref)

    def chunk(c, _):
        hb = hb_ref[...]
        g = jnp.dot(hb, wg_ref[c], preferred_element_type=F32)
        v = jnp.dot(hb, wv_ref[c], preferred_element_type=F32)
        gfull = jnp.concatenate([carry_ref[c], g], axis=0)
        carry_ref[c] = g[tm - halo:, :]
        cw = cw_ref[c]
        gc = cb_ref[c] + cw[0:1, :] * gfull[0:tm, :]
        gc = gc + cw[1:2, :] * gfull[SUBLANES:tm + SUBLANES, :]
        gc = gc + cw[2:3, :] * g
        hdn = jax.nn.gelu(gc) * v
        acc_ref[...] += jnp.dot(hdn.astype(BF16), wd_ref[c], preferred_element_type=F32)
        return 0

    lax.fori_loop(0, FF_NCHUNK, chunk, 0)
    o_ref[...] = x + _rms(acc_ref[...], gpost_ref[...])


def _ffn_layer(xp, gpre, wg, wv, cw, cb, wd, gpost):
    tm = TM_FFN
    kern = functools.partial(_ffn_kernel, tm=tm)
    halo = (CONV_WIDTH - 1) * SUBLANES
    return pl.pallas_call(
        kern,
        grid=(N_BLK, ROWS // tm),
        in_specs=[
            _row_spec(tm),
            _const_spec((1, D_MODEL)),
            _const_spec(wg.shape),
            _const_spec(wv.shape),
            _const_spec(cw.shape),
            _const_spec(cb.shape),
            _const_spec(wd.shape),
            _const_spec((1, D_MODEL)),
        ],
        out_specs=_row_spec(tm),
        out_shape=jax.ShapeDtypeStruct(xp.shape, F32),
        scratch_shapes=[
            pltpu.VMEM((tm, D_MODEL), BF16),
            pltpu.VMEM((FF_NCHUNK, halo, FF_CHUNK), F32),
            pltpu.VMEM((tm, D_MODEL), F32),
        ],
        compiler_params=pltpu.CompilerParams(
            dimension_semantics=("arbitrary", "arbitrary"),
            vmem_limit_bytes=VMEM_LIMIT),
        name="conv_ffn",
    )(xp, gpre, wg, wv, cw, cb, wd, gpost)


def _chunk_cols(w):
    k = w.shape[0]
    return jnp.transpose(w.reshape(k, FF_NCHUNK, FF_CHUNK), (1, 0, 2))


def kernel(x, s5_lambda_re, s5_lambda_im, s5_log_dt, s5_b_re, s5_b_im, s5_c_re, s5_c_im, s5_d, s5_w_glu, s5_b_glu, pool_w, pool_scale, ffn_w_gate, ffn_w_val, ffn_conv_w, ffn_conv_b, ffn_w_down, norm_mix_pre, norm_mix_post, norm_ffn_pre, norm_ffn_post):
    G, P, H = S5_GROUPS, S5_STATE, S5_GROUP_CH
    row = lambda v: v.reshape(1, D_MODEL)
    xp = jnp.transpose(x.reshape(N_BLK, SUBLANES, SEQ, D_MODEL), (0, 2, 1, 3)).reshape(N_BLK, ROWS, D_MODEL)

    for i in range(DEPTH):
        jl = i // 2
        if i % 2 == 0:
            a_re, a_im, bb_re, bb_im = _s5_prep(s5_lambda_re[jl], s5_lambda_im[jl], s5_log_dt[jl],
                                                 s5_b_re[jl], s5_b_im[jl])
            bmat = jnp.concatenate(
                [_block_diag(bb_re.reshape(S5_NJ, S5_JBLK, H, P)),
                 _block_diag(bb_im.reshape(S5_NJ, S5_JBLK, H, P))], axis=-1).astype(BF16)
            c_t = lambda c: _block_diag(
                jnp.transpose(c, (0, 2, 1)).reshape(S5_NJ, S5_JBLK, P, H)).astype(BF16)
            xp = _s5_layer(
                xp, row(norm_mix_pre[i]), bmat,
                a_re.reshape(S5_NJ, 1, S5_JST), a_im.reshape(S5_NJ, 1, S5_JST),
                c_t(s5_c_re[jl]), c_t(s5_c_im[jl]), row(s5_d[jl]),
                s5_w_glu[jl].astype(BF16), row(s5_b_glu[jl]), row(norm_mix_post[i]))
        else:
            xp = _pool_layer(xp, row(norm_mix_pre[i]), pool_w[jl].astype(BF16),
                             row(pool_scale[jl]), row(norm_mix_post[i]))
        xp = _ffn_layer(
            xp, row(norm_ffn_pre[i]),
            _chunk_cols(ffn_w_gate[i]).astype(BF16), _chunk_cols(ffn_w_val[i]).astype(BF16),
            _chunk_cols(ffn_conv_w[i]), ffn_conv_b[i].reshape(FF_NCHUNK, 1, FF_CHUNK),
            ffn_w_down[i].reshape(FF_NCHUNK, FF_CHUNK, D_MODEL).astype(BF16),
            row(norm_ffn_post[i]))

    return jnp.transpose(xp.reshape(N_BLK, SEQ, SUBLANES, D_MODEL), (0, 2, 1, 3)).reshape(BATCH, SEQ, D_MODEL)
```

```python
import functools
import math

import jax
import jax.numpy as jnp
from jax import lax
from jax.experimental import pallas as pl
from jax.experimental.pallas import tpu as pltpu

D_MODEL = 1024
BATCH = 32
SEQ = 2048
DEPTH = 4
S5_GROUP_CH = 16
S5_GROUPS = D_MODEL // S5_GROUP_CH
S5_STATE = 64
POOL_WINDOWS = (2, 4, 8, 16)
POOL_GROUP_CH = D_MODEL // len(POOL_WINDOWS)
D_FF = 2816
CONV_WIDTH = 3
RMS_EPS = 1e-6

SUBLANES = 8
LANES = 128

N_BLK = BATCH // SUBLANES
ROWS = SEQ * SUBLANES
S5_JBLK = 8
S5_NJ = S5_GROUPS // S5_JBLK
S5_JCH = S5_JBLK * S5_GROUP_CH
S5_JST = S5_JBLK * S5_STATE
FF_CHUNK = 256
FF_NCHUNK = D_FF // FF_CHUNK
POOL_HALO = (max(POOL_WINDOWS) - 1) * SUBLANES + SUBLANES

TM_S5 = 512
TM_POOL = 1024
TM_FFN = 512
VMEM_LIMIT = 56 * 1024 * 1024

F32 = jnp.float32
BF16 = jnp.bfloat16

GELU_C1 = math.sqrt(2.0 / math.pi)
GELU_C2 = GELU_C1 * 0.044715


def _rms(x, gain):
    return x * lax.rsqrt(jnp.mean(x * x, axis=-1, keepdims=True) + RMS_EPS) * gain


def _s5_prep_kernel(lr_ref, li_ref, ldt_ref, br_ref, bi_ref, are_ref, aim_ref, bbr_ref, bbi_ref):
    lr = lr_ref[...]
    li = li_ref[...]
    dt = jnp.exp(ldt_ref[...])
    mag = jnp.exp(lr * dt)
    ab_re = mag * jnp.cos(li * dt)
    ab_im = mag * jnp.sin(li * dt)
    den = lr * lr + li * li
    nr = ab_re - 1.0
    ni = ab_im
    f_re = (nr * lr + ni * li) / den
    f_im = (ni * lr - nr * li) / den
    are_ref[...] = ab_re
    aim_ref[...] = ab_im
    br = br_ref[...]
    bi = bi_ref[...]
    fr = f_re[:, None, :]
    fi = f_im[:, None, :]
    bbr_ref[...] = fr * br - fi * bi
    bbi_ref[...] = fr * bi + fi * br


def _s5_prep(lam_re, lam_im, log_dt, b_re, b_im):
    G, P, H = S5_GROUPS, S5_STATE, S5_GROUP_CH
    br_t = jnp.transpose(b_re, (0, 2, 1))
    bi_t = jnp.transpose(b_im, (0, 2, 1))
    return pl.pallas_call(
        _s5_prep_kernel,
        out_shape=(
            jax.ShapeDtypeStruct((G, P), F32),
            jax.ShapeDtypeStruct((G, P), F32),
            jax.ShapeDtypeStruct((G, H, P), F32),
            jax.ShapeDtypeStruct((G, H, P), F32),
        ),
        name="s5_prep",
    )(lam_re, lam_im, log_dt.reshape(G, 1), br_t, bi_t)


def _block_diag(m):
    nj, jb, r, c = m.shape
    eye = jnp.eye(jb, dtype=m.dtype)
    out = m[:, :, :, None, :] * eye[None, :, None, :, None]
    return out.reshape(nj, jb * r, jb * c)


def _s5_kernel(x_ref, gpre_ref, bmat_ref, are_ref, aim_ref, cr_ref, ci_ref, dskip_ref,
               wglu_ref, bglu_ref, gpost_ref, o_ref,
               ub_ref, xs_ref, y_ref, hst_ref, *, tm):
    j = pl.program_id(1)

    @pl.when(j == 0)
    def _():
        hst_ref[...] = jnp.zeros_like(hst_ref)

    x = x_ref[...]
    u = _rms(x, gpre_ref[...])
    ub_ref[...] = u.astype(BF16)
    nsteps = tm // SUBLANES

    for jb in range(S5_NJ):
        xs_ref[...] = jnp.dot(ub_ref[:, jb * S5_JCH:(jb + 1) * S5_JCH], bmat_ref[jb],
                              preferred_element_type=F32)
        ar = jnp.broadcast_to(are_ref[jb], (SUBLANES, S5_JST))
        ai = jnp.broadcast_to(aim_ref[jb], (SUBLANES, S5_JST))

        def step(t, carry):
            hr, hi = carry
            r0 = pl.multiple_of(t * SUBLANES, SUBLANES)
            xr = xs_ref[pl.ds(r0, SUBLANES), 0:S5_JST]
            xi = xs_ref[pl.ds(r0, SUBLANES), S5_JST:2 * S5_JST]
            nhr = ar * hr - ai * hi + xr
            nhi = ar * hi + ai * hr + xi
            xs_ref[pl.ds(r0, SUBLANES), 0:S5_JST] = nhr
            xs_ref[pl.ds(r0, SUBLANES), S5_JST:2 * S5_JST] = nhi
            return nhr, nhi

        h0 = (hst_ref[jb, :, 0:S5_JST], hst_ref[jb, :, S5_JST:2 * S5_JST])
        hr, hi = lax.fori_loop(0, nsteps, step, h0, unroll=8)
        hst_ref[jb, :, 0:S5_JST] = hr
        hst_ref[jb, :, S5_JST:2 * S5_JST] = hi

        yr = jnp.dot(xs_ref[:, 0:S5_JST].astype(BF16), cr_ref[jb], preferred_element_type=F32)
        yi = jnp.dot(xs_ref[:, S5_JST:2 * S5_JST].astype(BF16), ci_ref[jb], preferred_element_type=F32)
        y_ref[:, jb * S5_JCH:(jb + 1) * S5_JCH] = yr - yi

    y = y_ref[...] + dskip_ref[...] * u
    y = jax.nn.gelu(y)
    z = jnp.dot(y.astype(BF16), wglu_ref[...], preferred_element_type=F32) + bglu_ref[...]
    out = y * jax.nn.sigmoid(z)
    o_ref[...] = x + _rms(out, gpost_ref[...])


def _const_spec(shape):
    nd = len(shape)
    return pl.BlockSpec(shape, lambda i, j: (0,) * nd)


def _row_spec(tm):
    return pl.BlockSpec((None, tm, D_MODEL), lambda i, j: (i, j, 0))


def _s5_layer(xp, gpre, bmat, a_re, a_im, cr, ci, dskip, wglu, bglu, gpost):
    tm = TM_S5
    kern = functools.partial(_s5_kernel, tm=tm)
    return pl.pallas_call(
        kern,
        grid=(N_BLK, ROWS // tm),
        in_specs=[
            _row_spec(tm),
            _const_spec((1, D_MODEL)),
            _const_spec(bmat.shape),
            _const_spec(a_re.shape),
            _const_spec(a_im.shape),
            _const_spec(cr.shape),
            _const_spec(ci.shape),
            _const_spec((1, D_MODEL)),
            _const_spec(wglu.shape),
            _const_spec((1, D_MODEL)),
            _const_spec((1, D_MODEL)),
        ],
        out_specs=_row_spec(tm),
        out_shape=jax.ShapeDtypeStruct(xp.shape, F32),
        scratch_shapes=[
            pltpu.VMEM((tm, D_MODEL), BF16),
            pltpu.VMEM((tm, 2 * S5_JST), F32),
            pltpu.VMEM((tm, D_MODEL), F32),
            pltpu.VMEM((S5_NJ, SUBLANES, 2 * S5_JST), F32),
        ],
        compiler_params=pltpu.CompilerParams(
            dimension_semantics=("arbitrary", "arbitrary"),
            vmem_limit_bytes=VMEM_LIMIT),
        name="s5_mixer",
    )(xp, gpre, bmat, a_re, a_im, cr, ci, dskip, wglu, bglu, gpost)


def _pool_kernel(x_ref, gpre_ref, w_ref, scale_ref, gpost_ref, o_ref, buf_ref, m_ref, *, tm):
    j = pl.program_id(1)

    @pl.when(j == 0)
    def _():
        buf_ref[0:POOL_HALO, :] = jnp.zeros((POOL_HALO, D_MODEL), F32)

    @pl.when(j != 0)
    def _():
        buf_ref[0:POOL_HALO, :] = buf_ref[tm:tm + POOL_HALO, :]

    x = x_ref[...]
    u = _rms(x, gpre_ref[...])
    buf_ref[POOL_HALO:POOL_HALO + tm, :] = u

    row = lax.broadcasted_iota(jnp.int32, (tm, 1), 0)
    t1 = (j * (tm // SUBLANES) + row // SUBLANES + 1).astype(F32)

    for g, win in enumerate(POOL_WINDOWS):
        lo, hi = g * POOL_GROUP_CH, (g + 1) * POOL_GROUP_CH
        s = buf_ref[:, lo:hi]
        step = 1
        while step < win:
            sh = step * SUBLANES
            s = s[sh:, :] + s[:-sh, :]
            step *= 2
        s = s[s.shape[0] - tm:, :]
        cnt = jnp.minimum(t1, float(win))
        diff = s / cnt - u[:, lo:hi]
        m_ref[:, lo:hi] = jnp.dot(diff.astype(BF16), w_ref[g], preferred_element_type=F32)

    m = m_ref[...] * scale_ref[...]
    o_ref[...] = x + _rms(m, gpost_ref[...])


def _pool_layer(xp, gpre, w, scale, gpost):
    tm = TM_POOL
    kern = functools.partial(_pool_kernel, tm=tm)
    return pl.pallas_call(
        kern,
        grid=(N_BLK, ROWS // tm),
        in_specs=[
            _row_spec(tm),
            _const_spec((1, D_MODEL)),
            _const_spec(w.shape),
            _const_spec((1, D_MODEL)),
            _const_spec((1, D_MODEL)),
        ],
        out_specs=_row_spec(tm),
        out_shape=jax.ShapeDtypeStruct(xp.shape, F32),
        scratch_shapes=[
            pltpu.VMEM((POOL_HALO + tm, D_MODEL), F32),
            pltpu.VMEM((tm, D_MODEL), F32),
        ],
        compiler_params=pltpu.CompilerParams(
            dimension_semantics=("arbitrary", "arbitrary"),
            vmem_limit_bytes=VMEM_LIMIT),
        name="pool_mixer",
    )(xp, gpre, w, scale, gpost)


def _ffn_kernel(x_ref, gpre_ref, wg_ref, wv_ref, cw_ref, cb_ref, wd_ref, gpost_ref, o_ref,
                hb_ref, carry_ref, g_ref, v_ref, hdn_ref, *, tm):
    j = pl.program_id(1)
    halo = (CONV_WIDTH - 1) * SUBLANES

    @pl.when(j == 0)
    def _():
        carry_ref[...] = jnp.zeros_like(carry_ref)

    x = x_ref[...]
    hb_ref[...] = _rms(x, gpre_ref[...]).astype(BF16)

    for c in range(FF_NCHUNK):
        s = c % 2
        g_ref[s, 0:halo, :] = carry_ref[c]
        g_ref[s, halo:halo + tm, :] = jnp.dot(hb_ref[...], wg_ref[c], preferred_element_type=F32)
        v_ref[s] = jnp.dot(hb_ref[...], wv_ref[c], preferred_element_type=F32)
        carry_ref[c] = g_ref[s, tm:tm + halo, :]
        cw = cw_ref[c]
        gc = cb_ref[c] + cw[0:1, :] * g_ref[s, 0:tm, :]
        gc = gc + cw[1:2, :] * g_ref[s, SUBLANES:tm + SUBLANES, :]
        gc = gc + cw[2:3, :] * g_ref[s, halo:halo + tm, :]
        inner = gc * (GELU_C1 + GELU_C2 * (gc * gc))
        hdn = (gc * v_ref[s]) * (1.0 + jnp.tanh(inner))
        hdn_ref[:, c * FF_CHUNK:(c + 1) * FF_CHUNK] = hdn.astype(BF16)

    f = jnp.dot(hdn_ref[...], wd_ref[...], preferred_element_type=F32)
    o_ref[...] = x + _rms(f, gpost_ref[...])


def _ffn_layer(xp, gpre, wg, wv, cw, cb, wd, gpost):
    tm = TM_FFN
    kern = functools.partial(_ffn_kernel, tm=tm)
    halo = (CONV_WIDTH - 1) * SUBLANES
    return pl.pallas_call(
        kern,
        grid=(N_BLK, ROWS // tm),
        in_specs=[
            _row_spec(tm),
            _const_spec((1, D_MODEL)),
            _const_spec(wg.shape),
            _const_spec(wv.shape),
            _const_spec(cw.shape),
            _const_spec(cb.shape),
            _const_spec(wd.shape),
            _const_spec((1, D_MODEL)),
        ],
        out_specs=_row_spec(tm),
        out_shape=jax.ShapeDtypeStruct(xp.shape, F32),
        scratch_shapes=[
            pltpu.VMEM((tm, D_MODEL), BF16),
            pltpu.VMEM((FF_NCHUNK, halo, FF_CHUNK), F32),
            pltpu.VMEM((2, halo + tm, FF_CHUNK), F32),
            pltpu.VMEM((2, tm, FF_CHUNK), F32),
            pltpu.VMEM((tm, D_FF), BF16),
        ],
        compiler_params=pltpu.CompilerParams(
            dimension_semantics=("arbitrary", "arbitrary"),
            vmem_limit_bytes=VMEM_LIMIT),
        name="conv_ffn",
    )(xp, gpre, wg, wv, cw, cb, wd, gpost)


def _chunk_cols(w):
    k = w.shape[0]
    return jnp.transpose(w.reshape(k, FF_NCHUNK, FF_CHUNK), (1, 0, 2))


def kernel(x, s5_lambda_re, s5_lambda_im, s5_log_dt, s5_b_re, s5_b_im, s5_c_re, s5_c_im, s5_d, s5_w_glu, s5_b_glu, pool_w, pool_scale, ffn_w_gate, ffn_w_val, ffn_conv_w, ffn_conv_b, ffn_w_down, norm_mix_pre, norm_mix_post, norm_ffn_pre, norm_ffn_post):
    G, P, H = S5_GROUPS, S5_STATE, S5_GROUP_CH
    row = lambda v: v.reshape(1, D_MODEL)
    xp = jnp.transpose(x.reshape(N_BLK, SUBLANES, SEQ, D_MODEL), (0, 2, 1, 3)).reshape(N_BLK, ROWS, D_MODEL)

    for i in range(DEPTH):
        jl = i // 2
        if i % 2 == 0:
            a_re, a_im, bb_re, bb_im = _s5_prep(s5_lambda_re[jl], s5_lambda_im[jl], s5_log_dt[jl],
                                                 s5_b_re[jl], s5_b_im[jl])
            bmat = jnp.concatenate(
                [_block_diag(bb_re.reshape(S5_NJ, S5_JBLK, H, P)),
                 _block_diag(bb_im.reshape(S5_NJ, S5_JBLK, H, P))], axis=-1).astype(BF16)
            c_t = lambda c: _block_diag(
                jnp.transpose(c, (0, 2, 1)).reshape(S5_NJ, S5_JBLK, P, H)).astype(BF16)
            xp = _s5_layer(
                xp, row(norm_mix_pre[i]), bmat,
                a_re.reshape(S5_NJ, 1, S5_JST), a_im.reshape(S5_NJ, 1, S5_JST),
                c_t(s5_c_re[jl]), c_t(s5_c_im[jl]), row(s5_d[jl]),
                s5_w_glu[jl].astype(BF16), row(s5_b_glu[jl]), row(norm_mix_post[i]))
        else:
            xp = _pool_layer(xp, row(norm_mix_pre[i]), pool_w[jl].astype(BF16),
                             row(pool_scale[jl]), row(norm_mix_post[i]))
        xp = _ffn_layer(
            xp, row(norm_ffn_pre[i]),
            _chunk_cols(ffn_w_gate[i]).astype(BF16), _chunk_cols(ffn_w_val[i]).astype(BF16),
            _chunk_cols(ffn_conv_w[i]), ffn_conv_b[i].reshape(FF_NCHUNK, 1, FF_CHUNK),
            (0.5 * ffn_w_down[i]).astype(BF16),
            row(norm_ffn_post[i]))

    return jnp.transpose(xp.reshape(N_BLK, SEQ, SUBLANES, D_MODEL), (0, 2, 1, 3)).reshape(BATCH, SEQ, D_MODEL)
```

```python
import functools
import math

import jax
import jax.numpy as jnp
from jax import lax
from jax.experimental import pallas as pl
from jax.experimental.pallas import tpu as pltpu

D_MODEL = 1024
BATCH = 32
SEQ = 2048
DEPTH = 4
S5_GROUP_CH = 16
S5_GROUPS = D_MODEL // S5_GROUP_CH
S5_STATE = 64
POOL_WINDOWS = (2, 4, 8, 16)
POOL_GROUP_CH = D_MODEL // len(POOL_WINDOWS)
D_FF = 2816
CONV_WIDTH = 3
RMS_EPS = 1e-6

SUBLANES = 8
LANES = 128

N_BLK = BATCH // SUBLANES
ROWS = SEQ * SUBLANES
S5_JBLK = LANES // S5_GROUP_CH
S5_NJ = S5_GROUPS // S5_JBLK
S5_JCH = S5_JBLK * S5_GROUP_CH
S5_JST = S5_JBLK * S5_STATE
S5_TB = 4
FF_CHUNK = 256
FF_NCHUNK = D_FF // FF_CHUNK
POOL_HALO = (max(POOL_WINDOWS) - 1) * SUBLANES + SUBLANES

TM_S5 = 1024
TM_POOL = 1024
TM_FFN = 512
VMEM_LIMIT = 60 * 1024 * 1024

F32 = jnp.float32
BF16 = jnp.bfloat16

GELU_C1 = math.sqrt(2.0 / math.pi)
GELU_C2 = GELU_C1 * 0.044715


def _rms(x, gain):
    return x * lax.rsqrt(jnp.mean(x * x, axis=-1, keepdims=True) + RMS_EPS) * gain


def _cmul(ar, ai, br, bi):
    return ar * br - ai * bi, ar * bi + ai * br


def _s5_prep_kernel(lr_ref, li_ref, ldt_ref, br_ref, bi_ref, cr_ref, ci_ref,
                    a4r_ref, a4i_ref, pinr_ref, pini_ref, car_ref, cai_ref, tt_ref):
    lr = lr_ref[...]
    li = li_ref[...]
    dt = jnp.exp(ldt_ref[...])
    mag = jnp.exp(lr * dt)
    ab_re = mag * jnp.cos(li * dt)
    ab_im = mag * jnp.sin(li * dt)
    den = lr * lr + li * li
    nr = ab_re - 1.0
    ni = ab_im
    f_re = (nr * lr + ni * li) / den
    f_im = (ni * lr - nr * li) / den
    over_h = lambda v: v[:, None, :]
    bbr, bbi = _cmul(over_h(f_re), over_h(f_im), br_ref[...], bi_ref[...])

    pw = [None, (ab_re, ab_im)]
    for _ in range(S5_TB - 1):
        pw.append(_cmul(pw[-1][0], pw[-1][1], ab_re, ab_im))
    a4r_ref[...] = pw[S5_TB][0]
    a4i_ref[...] = pw[S5_TB][1]

    for k in range(S5_TB):
        d = S5_TB - 1 - k
        if d == 0:
            pr, pi = bbr, bbi
        else:
            pr, pi = _cmul(over_h(pw[d][0]), over_h(pw[d][1]), bbr, bbi)
        pinr_ref[k] = pr
        pini_ref[k] = pi

    cr = cr_ref[...]
    ci = ci_ref[...]
    ca = [(cr, ci)]
    for d in range(1, S5_TB + 1):
        ca.append(_cmul(over_h(pw[d][0]), over_h(pw[d][1]), cr, ci))
    for k in range(S5_TB):
        car_ref[k] = ca[k + 1][0]
        cai_ref[k] = ca[k + 1][1]

    bdot = lambda x, y: lax.dot_general(x, y, (((2,), (2,)), ((0,), (0,))),
                                        precision=lax.Precision.HIGHEST,
                                        preferred_element_type=F32)
    for d in range(S5_TB):
        tt_ref[d] = bdot(bbr, ca[d][0]) - bdot(bbi, ca[d][1])


def _s5_prep(lam_re, lam_im, log_dt, b_re, b_im, c_re, c_im):
    G, P, H = S5_GROUPS, S5_STATE, S5_GROUP_CH
    br_t = jnp.transpose(b_re, (0, 2, 1))
    bi_t = jnp.transpose(b_im, (0, 2, 1))
    ghp = jax.ShapeDtypeStruct((S5_TB, G, H, P), F32)
    return pl.pallas_call(
        _s5_prep_kernel,
        out_shape=(
            jax.ShapeDtypeStruct((G, P), F32),
            jax.ShapeDtypeStruct((G, P), F32),
            ghp, ghp, ghp, ghp,
            jax.ShapeDtypeStruct((S5_TB, G, H, H), F32),
        ),
        name="s5_prep",
    )(lam_re, lam_im, log_dt.reshape(G, 1), br_t, bi_t, c_re, c_im)


def _block_diag(m):
    *lead, jb, r, c = m.shape
    eye = jnp.eye(jb, dtype=m.dtype)
    out = m[..., :, :, None, :] * eye[:, None, :, None]
    return out.reshape(*lead, jb * r, jb * c)


def _s5_operators(a4r, a4i, pinr, pini, car, cai, tt):
    G, P, H, TB = S5_GROUPS, S5_STATE, S5_GROUP_CH, S5_TB

    def in_map(p):
        bd = _block_diag(p.reshape(TB, S5_NJ, S5_JBLK, H, P))
        return jnp.moveaxis(bd, 0, 1).reshape(S5_NJ, TB * S5_JCH, S5_JST)

    def out_map(c):
        ct = jnp.transpose(c, (0, 1, 3, 2)).reshape(TB, S5_NJ, S5_JBLK, P, H)
        bd = _block_diag(ct)
        return jnp.transpose(bd, (1, 2, 0, 3)).reshape(S5_NJ, S5_JST, TB * S5_JCH)

    bin_ = jnp.concatenate([in_map(pinr), in_map(pini)], axis=-1).astype(BF16)
    csr = out_map(car).astype(BF16)
    csi = out_map(cai).astype(BF16)
    ttbd = _block_diag(tt.reshape(TB, S5_NJ, S5_JBLK, H, H))
    zero = jnp.zeros_like(ttbd[0])
    rows = [jnp.concatenate([ttbd[k - m] if k >= m else zero for k in range(TB)], axis=-1)
            for m in range(TB)]
    tmat = jnp.concatenate(rows, axis=-2).astype(BF16)
    a4r = a4r.reshape(S5_NJ, 1, S5_JST)
    a4i = a4i.reshape(S5_NJ, 1, S5_JST)
    return bin_, a4r, a4i, csr, csi, tmat


def _s5_kernel(x_ref, gpre_ref, bin_ref, a4r_ref, a4i_ref, csr_ref, csi_ref, tmat_ref, dskip_ref,
               wglu_ref, bglu_ref, gpost_ref, o_ref,
               u4_ref, ub4_ref, xs_ref, y4_ref, hst_ref, *, tm):
    j = pl.program_id(1)
    m = tm // S5_TB
    nq = m // SUBLANES
    blk = S5_TB * SUBLANES

    @pl.when(j == 0)
    def _():
        hst_ref[...] = jnp.zeros_like(hst_ref)

    gpre = gpre_ref[...]
    for k in range(S5_TB):
        xk = jnp.concatenate(
            [x_ref[q * blk + k * SUBLANES:q * blk + (k + 1) * SUBLANES, :] for q in range(nq)], axis=0)
        uk = _rms(xk, gpre)
        u4_ref[k * m:(k + 1) * m, :] = uk
        ub4_ref[k * m:(k + 1) * m, :] = uk.astype(BF16)

    for jb in range(S5_NJ):
        s = jb % 2
        lo, hi_ = jb * S5_JCH, (jb + 1) * S5_JCH
        lhs = jnp.concatenate([ub4_ref[k * m:(k + 1) * m, lo:hi_] for k in range(S5_TB)], axis=1)
        xs_ref[s, SUBLANES:SUBLANES + m, :] = jnp.dot(lhs, bin_ref[jb], preferred_element_type=F32)
        ar = jnp.broadcast_to(a4r_ref[jb], (SUBLANES, S5_JST))
        ai = jnp.broadcast_to(a4i_ref[jb], (SUBLANES, S5_JST))
        hr = hst_ref[jb, :, 0:S5_JST]
        hi = hst_ref[jb, :, S5_JST:2 * S5_JST]
        xs_ref[s, 0:SUBLANES, 0:S5_JST] = hr
        xs_ref[s, 0:SUBLANES, S5_JST:2 * S5_JST] = hi
        for q in range(nq):
            r0 = (q + 1) * SUBLANES
            xr = xs_ref[s, r0:r0 + SUBLANES, 0:S5_JST]
            xi = xs_ref[s, r0:r0 + SUBLANES, S5_JST:2 * S5_JST]
            hr, hi = ar * hr - ai * hi + xr, ar * hi + ai * hr + xi
            xs_ref[s, r0:r0 + SUBLANES, 0:S5_JST] = hr
            xs_ref[s, r0:r0 + SUBLANES, S5_JST:2 * S5_JST] = hi
        hst_ref[jb, :, 0:S5_JST] = hr
        hst_ref[jb, :, S5_JST:2 * S5_JST] = hi

        y4j = (jnp.dot(xs_ref[s, 0:m, 0:S5_JST].astype(BF16), csr_ref[jb], preferred_element_type=F32)
               - jnp.dot(xs_ref[s, 0:m, S5_JST:2 * S5_JST].astype(BF16), csi_ref[jb], preferred_element_type=F32)
               + jnp.dot(lhs, tmat_ref[jb], preferred_element_type=F32))
        for k in range(S5_TB):
            y4_ref[k * m:(k + 1) * m, lo:hi_] = y4j[:, k * S5_JCH:(k + 1) * S5_JCH]

    dskip = dskip_ref[...]
    bglu = bglu_ref[...]
    gpost = gpost_ref[...]
    for k in range(S5_TB):
        y = y4_ref[k * m:(k + 1) * m, :] + dskip * u4_ref[k * m:(k + 1) * m, :]
        y = jax.nn.gelu(y)
        z = jnp.dot(y.astype(BF16), wglu_ref[...], preferred_element_type=F32) + bglu
        res = _rms(y * jax.nn.sigmoid(z), gpost)
        for q in range(nq):
            r = q * blk + k * SUBLANES
            o_ref[r:r + SUBLANES, :] = x_ref[r:r + SUBLANES, :] + res[q * SUBLANES:(q + 1) * SUBLANES, :]


def _const_spec(shape):
    nd = len(shape)
    return pl.BlockSpec(shape, lambda i, j: (0,) * nd, pipeline_mode=pl.Buffered(1))


def _row_spec(tm):
    return pl.BlockSpec((None, tm, D_MODEL), lambda i, j: (i, j, 0))


def _s5_layer(xp, gpre, bin_, a4r, a4i, csr, csi, tmat, dskip, wglu, bglu, gpost):
    tm = TM_S5
    m = tm // S5_TB
    kern = functools.partial(_s5_kernel, tm=tm)
    consts = (gpre, bin_, a4r, a4i, csr, csi, tmat, dskip, wglu, bglu, gpost)
    return pl.pallas_call(
        kern,
        grid=(N_BLK, ROWS // tm),
        in_specs=[_row_spec(tm)] + [_const_spec(c.shape) for c in consts],
        out_specs=_row_spec(tm),
        out_shape=jax.ShapeDtypeStruct(xp.shape, F32),
        scratch_shapes=[
            pltpu.VMEM((tm, D_MODEL), F32),
            pltpu.VMEM((tm, D_MODEL), BF16),
            pltpu.VMEM((2, SUBLANES + m, 2 * S5_JST), F32),
            pltpu.VMEM((tm, D_MODEL), F32),
            pltpu.VMEM((S5_NJ, SUBLANES, 2 * S5_JST), F32),
        ],
        compiler_params=pltpu.CompilerParams(
            dimension_semantics=("arbitrary", "arbitrary"),
            vmem_limit_bytes=VMEM_LIMIT),
        name="s5_mixer",
    )(xp, *consts)


def _pool_kernel(x_ref, gpre_ref, w_ref, scale_ref, gpost_ref, o_ref, buf_ref, m_ref, *, tm):
    j = pl.program_id(1)

    @pl.when(j == 0)
    def _():
        buf_ref[0:POOL_HALO, :] = jnp.zeros((POOL_HALO, D_MODEL), F32)

    @pl.when(j != 0)
    def _():
        buf_ref[0:POOL_HALO, :] = buf_ref[tm:tm + POOL_HALO, :]

    x = x_ref[...]
    u = _rms(x, gpre_ref[...])
    buf_ref[POOL_HALO:POOL_HALO + tm, :] = u

    row = lax.broadcasted_iota(jnp.int32, (tm, 1), 0)
    t1 = (j * (tm // SUBLANES) + row // SUBLANES + 1).astype(F32)

    for g, win in enumerate(POOL_WINDOWS):
        lo, hi = g * POOL_GROUP_CH, (g + 1) * POOL_GROUP_CH
        s = buf_ref[:, lo:hi]
        step = 1
        while step < win:
            sh = step * SUBLANES
            s = s[sh:, :] + s[:-sh, :]
            step *= 2
        s = s[s.shape[0] - tm:, :]
        cnt = jnp.minimum(t1, float(win))
        diff = s / cnt - u[:, lo:hi]
        m_ref[:, lo:hi] = jnp.dot(diff.astype(BF16), w_ref[g], preferred_element_type=F32)

    m = m_ref[...] * scale_ref[...]
    o_ref[...] = x + _rms(m, gpost_ref[...])


def _pool_layer(xp, gpre, w, scale, gpost):
    tm = TM_POOL
    kern = functools.partial(_pool_kernel, tm=tm)
    consts = (gpre, w, scale, gpost)
    return pl.pallas_call(
        kern,
        grid=(N_BLK, ROWS // tm),
        in_specs=[_row_spec(tm)] + [_const_spec(c.shape) for c in consts],
        out_specs=_row_spec(tm),
        out_shape=jax.ShapeDtypeStruct(xp.shape, F32),
        scratch_shapes=[
            pltpu.VMEM((POOL_HALO + tm, D_MODEL), F32),
            pltpu.VMEM((tm, D_MODEL), F32),
        ],
        compiler_params=pltpu.CompilerParams(
            dimension_semantics=("arbitrary", "arbitrary"),
            vmem_limit_bytes=VMEM_LIMIT),
        name="pool_mixer",
    )(xp, *consts)


def _ffn_kernel(x_ref, gpre_ref, wg_ref, wv_ref, cw_ref, cb_ref, wd_ref, gpost_ref, o_ref,
                hb_ref, carry_ref, g_ref, v_ref, hdn_ref, *, tm):
    j = pl.program_id(1)
    halo = (CONV_WIDTH - 1) * SUBLANES

    @pl.when(j == 0)
    def _():
        carry_ref[...] = jnp.zeros_like(carry_ref)

    x = x_ref[...]
    hb_ref[...] = _rms(x, gpre_ref[...]).astype(BF16)

    for c in range(FF_NCHUNK):
        s = c % 2
        g_ref[s, 0:halo, :] = carry_ref[c]
        g_ref[s, halo:halo + tm, :] = jnp.dot(hb_ref[...], wg_ref[c], preferred_element_type=F32)
        v_ref[s] = jnp.dot(hb_ref[...], wv_ref[c], preferred_element_type=F32)
        carry_ref[c] = g_ref[s, tm:tm + halo, :]
        cw = cw_ref[c]
        gc = cb_ref[c] + cw[0:1, :] * g_ref[s, 0:tm, :]
        gc = gc + cw[1:2, :] * g_ref[s, SUBLANES:tm + SUBLANES, :]
        gc = gc + cw[2:3, :] * g_ref[s, halo:halo + tm, :]
        inner = gc * (GELU_C1 + GELU_C2 * (gc * gc))
        hdn = (gc * v_ref[s]) * (1.0 + jnp.tanh(inner))
        hdn_ref[:, c * FF_CHUNK:(c + 1) * FF_CHUNK] = hdn.astype(BF16)

    f = jnp.dot(hdn_ref[...], wd_ref[...], preferred_element_type=F32)
    o_ref[...] = x + _rms(f, gpost_ref[...])


def _ffn_layer(xp, gpre, wg, wv, cw, cb, wd_half, gpost):
    tm = TM_FFN
    kern = functools.partial(_ffn_kernel, tm=tm)
    halo = (CONV_WIDTH - 1) * SUBLANES
    consts = (gpre, wg, wv, cw, cb, wd_half, gpost)
    return pl.pallas_call(
        kern,
        grid=(N_BLK, ROWS // tm),
        in_specs=[_row_spec(tm)] + [_const_spec(c.shape) for c in consts],
        out_specs=_row_spec(tm),
        out_shape=jax.ShapeDtypeStruct(xp.shape, F32),
        scratch_shapes=[
            pltpu.VMEM((tm, D_MODEL), BF16),
            pltpu.VMEM((FF_NCHUNK, halo, FF_CHUNK), F32),
            pltpu.VMEM((2, halo + tm, FF_CHUNK), F32),
            pltpu.VMEM((2, tm, FF_CHUNK), F32),
            pltpu.VMEM((tm, D_FF), BF16),
        ],
        compiler_params=pltpu.CompilerParams(
            dimension_semantics=("arbitrary", "arbitrary"),
            vmem_limit_bytes=VMEM_LIMIT),
        name="conv_ffn",
    )(xp, *consts)


def _chunk_cols(w):
    k = w.shape[0]
    return jnp.transpose(w.reshape(k, FF_NCHUNK, FF_CHUNK), (1, 0, 2))


def kernel(x, s5_lambda_re, s5_lambda_im, s5_log_dt, s5_b_re, s5_b_im, s5_c_re, s5_c_im, s5_d, s5_w_glu, s5_b_glu, pool_w, pool_scale, ffn_w_gate, ffn_w_val, ffn_conv_w, ffn_conv_b, ffn_w_down, norm_mix_pre, norm_mix_post, norm_ffn_pre, norm_ffn_post):
    row = lambda v: v.reshape(1, D_MODEL)
    xp = jnp.transpose(x.reshape(N_BLK, SUBLANES, SEQ, D_MODEL), (0, 2, 1, 3)).reshape(N_BLK, ROWS, D_MODEL)

    for i in range(DEPTH):
        jl = i // 2
        if i % 2 == 0:
            prep = _s5_prep(s5_lambda_re[jl], s5_lambda_im[jl], s5_log_dt[jl],
                            s5_b_re[jl], s5_b_im[jl], s5_c_re[jl], s5_c_im[jl])
            bin_, a4r, a4i, csr, csi, tmat = _s5_operators(*prep)
            xp = _s5_layer(
                xp, row(norm_mix_pre[i]), bin_, a4r, a4i, csr, csi, tmat, row(s5_d[jl]),
                s5_w_glu[jl].astype(BF16), row(s5_b_glu[jl]), row(norm_mix_post[i]))
        else:
            xp = _pool_layer(xp, row(norm_mix_pre[i]), pool_w[jl].astype(BF16),
                             row(pool_scale[jl]), row(norm_mix_post[i]))
        xp = _ffn_layer(
            xp, row(norm_ffn_pre[i]),
            _chunk_cols(ffn_w_gate[i]).astype(BF16), _chunk_cols(ffn_w_val[i]).astype(BF16),
            _chunk_cols(ffn_conv_w[i]), ffn_conv_b[i].reshape(FF_NCHUNK, 1, FF_CHUNK),
            (0.5 * ffn_w_down[i]).astype(BF16),
            row(norm_ffn_post[i]))

    return jnp.transpose(xp.reshape(N_BLK, SEQ, SUBLANES, D_MODEL), (0, 2, 1, 3)).reshape(BATCH, SEQ, D_MODEL)
```

```python
import functools
import math

import jax
import jax.numpy as jnp
from jax import lax
from jax.experimental import pallas as pl
from jax.experimental.pallas import tpu as pltpu

D_MODEL = 1024
BATCH = 32
SEQ = 2048
DEPTH = 4
S5_GROUP_CH = 16
S5_GROUPS = D_MODEL // S5_GROUP_CH
S5_STATE = 64
POOL_WINDOWS = (2, 4, 8, 16)
POOL_GROUP_CH = D_MODEL // len(POOL_WINDOWS)
D_FF = 2816
CONV_WIDTH = 3
RMS_EPS = 1e-6

SUBLANES = 8
LANES = 128

N_BLK = BATCH // SUBLANES
ROWS = SEQ * SUBLANES
S5_JBLK = LANES // S5_GROUP_CH
S5_NJ = S5_GROUPS // S5_JBLK
S5_JCH = S5_JBLK * S5_GROUP_CH
S5_JST = S5_JBLK * S5_STATE
S5_TB = 2
FF_CHUNK = 256
FF_NCHUNK = D_FF // FF_CHUNK
POOL_HALO = (max(POOL_WINDOWS) - 1) * SUBLANES + SUBLANES

TM_S5 = 1024
TM_POOL = 1024
TM_FFN = 512
VMEM_LIMIT = 60 * 1024 * 1024

F32 = jnp.float32
BF16 = jnp.bfloat16

GELU_C1 = math.sqrt(2.0 / math.pi)
GELU_C2 = GELU_C1 * 0.044715


def _rms(x, gain):
    return x * lax.rsqrt(jnp.mean(x * x, axis=-1, keepdims=True) + RMS_EPS) * gain


def _cmul(ar, ai, br, bi):
    return ar * br - ai * bi, ar * bi + ai * br


def _s5_prep_kernel(lr_ref, li_ref, ldt_ref, br_ref, bi_ref, cr_ref, ci_ref,
                    ab_ref, pinr_ref, pini_ref, car_ref, cai_ref, tt_ref):
    lr = lr_ref[...]
    li = li_ref[...]
    dt = jnp.exp(ldt_ref[...])
    mag = jnp.exp(lr * dt)
    ab_re = mag * jnp.cos(li * dt)
    ab_im = mag * jnp.sin(li * dt)
    den = lr * lr + li * li
    nr = ab_re - 1.0
    ni = ab_im
    f_re = (nr * lr + ni * li) / den
    f_im = (ni * lr - nr * li) / den
    over_h = lambda v: v[:, None, :]
    bbr, bbi = _cmul(over_h(f_re), over_h(f_im), br_ref[...], bi_ref[...])

    pw = [None, (ab_re, ab_im)]
    for _ in range(S5_TB - 1):
        pw.append(_cmul(pw[-1][0], pw[-1][1], ab_re, ab_im))
    ab_ref[0] = pw[S5_TB][0]
    ab_ref[1] = pw[S5_TB][1]

    for k in range(S5_TB):
        d = S5_TB - 1 - k
        if d == 0:
            pr, pi = bbr, bbi
        else:
            pr, pi = _cmul(over_h(pw[d][0]), over_h(pw[d][1]), bbr, bbi)
        pinr_ref[k] = pr
        pini_ref[k] = pi

    cr = cr_ref[...]
    ci = ci_ref[...]
    ca = [(cr, ci)]
    for d in range(1, S5_TB + 1):
        ca.append(_cmul(over_h(pw[d][0]), over_h(pw[d][1]), cr, ci))
    for k in range(S5_TB):
        car_ref[k] = ca[k + 1][0]
        cai_ref[k] = ca[k + 1][1]

    bdot = lambda x, y: lax.dot_general(x, y, (((2,), (2,)), ((0,), (0,))),
                                        precision=lax.Precision.HIGHEST,
                                        preferred_element_type=F32)
    for d in range(S5_TB):
        tt_ref[d] = bdot(bbr, ca[d][0]) - bdot(bbi, ca[d][1])


def _s5_prep(lam_re, lam_im, log_dt, b_re, b_im, c_re, c_im):
    G, P, H = S5_GROUPS, S5_STATE, S5_GROUP_CH
    br_t = jnp.transpose(b_re, (0, 2, 1))
    bi_t = jnp.transpose(b_im, (0, 2, 1))
    ghp = jax.ShapeDtypeStruct((S5_TB, G, H, P), F32)
    return pl.pallas_call(
        _s5_prep_kernel,
        out_shape=(
            jax.ShapeDtypeStruct((2, G, P), F32),
            ghp, ghp, ghp, ghp,
            jax.ShapeDtypeStruct((S5_TB, G, H, H), F32),
        ),
        name="s5_prep",
    )(lam_re, lam_im, log_dt.reshape(G, 1), br_t, bi_t, c_re, c_im)


def _block_diag(m):
    *lead, jb, r, c = m.shape
    eye = jnp.eye(jb, dtype=m.dtype)
    out = m[..., :, :, None, :] * eye[:, None, :, None]
    return out.reshape(*lead, jb * r, jb * c)


def _s5_operators(ab, pinr, pini, car, cai, tt):
    G, P, H, TB = S5_GROUPS, S5_STATE, S5_GROUP_CH, S5_TB

    def in_map(p):
        bd = _block_diag(p.reshape(TB, S5_NJ, S5_JBLK, H, P))
        return jnp.moveaxis(bd, 0, 1).reshape(S5_NJ, TB * S5_JCH, S5_JST)

    def out_map(c):
        ct = jnp.transpose(c, (0, 1, 3, 2)).reshape(TB, S5_NJ, S5_JBLK, P, H)
        bd = _block_diag(ct)
        return jnp.transpose(bd, (1, 2, 0, 3)).reshape(S5_NJ, S5_JST, TB * S5_JCH)

    bin_ = jnp.concatenate([in_map(pinr), in_map(pini)], axis=-1).astype(BF16)
    csr = out_map(car).astype(BF16)
    csi = out_map(cai).astype(BF16)
    ttbd = _block_diag(tt.reshape(TB, S5_NJ, S5_JBLK, H, H))
    zero = jnp.zeros_like(ttbd[0])
    rows = [jnp.concatenate([ttbd[k - m] if k >= m else zero for k in range(TB)], axis=-1)
            for m in range(TB)]
    tmat = jnp.concatenate(rows, axis=-2).astype(BF16)
    return bin_, ab.reshape(2, S5_NJ, 1, S5_JST), csr, csi, tmat


def _s5_kernel(x_ref, gpre_ref, bin_ref, ab_ref, csr_ref, csi_ref, tmat_ref, dskip_ref,
               wglu_ref, bglu_ref, gpost_ref, o_ref,
               x4_ref, u4_ref, ub4_ref, xs_ref, y4_ref, hst_ref, *, tm, x_seq_major):
    j = pl.program_id(1)
    m = tm // S5_TB
    nq = m // SUBLANES
    blk = S5_TB * SUBLANES

    @pl.when(j == 0)
    def _():
        hst_ref[...] = jnp.zeros_like(hst_ref)

    gpre = gpre_ref[...]
    def x_step(t):
        if x_seq_major:
            return x_ref[:, t, :]
        return x_ref[t * SUBLANES:(t + 1) * SUBLANES, :]

    for k in range(S5_TB):
        xk = jnp.concatenate([x_step(q * S5_TB + k) for q in range(nq)], axis=0)
        x4_ref[k * m:(k + 1) * m, :] = xk
        uk = _rms(xk, gpre)
        u4_ref[k * m:(k + 1) * m, :] = uk
        ub4_ref[k * m:(k + 1) * m, :] = uk.astype(BF16)

    def block_inputs(jb):
        lo = jb * S5_JCH
        return jnp.concatenate([ub4_ref[k * m:(k + 1) * m, lo:lo + S5_JCH] for k in range(S5_TB)], axis=1)

    def in_proj(jb):
        xs_ref[jb % 2, SUBLANES:SUBLANES + m, :] = jnp.dot(block_inputs(jb), bin_ref[jb],
                                                             preferred_element_type=F32)

    in_proj(0)
    for jb in range(S5_NJ):
        s = jb % 2
        lo, hi_ = jb * S5_JCH, (jb + 1) * S5_JCH
        if jb + 1 < S5_NJ:
            in_proj(jb + 1)
        lhs = block_inputs(jb)
        ar, ai = (jnp.broadcast_to(ab_ref[n, jb], (SUBLANES, S5_JST)) for n in range(2))
        re, im = slice(0, S5_JST), slice(S5_JST, 2 * S5_JST)
        hr = hst_ref[jb, :, re]
        hi = hst_ref[jb, :, im]
        xs_ref[s, 0:SUBLANES, re] = hr
        xs_ref[s, 0:SUBLANES, im] = hi
        for q in range(nq):
            r0 = slice((q + 1) * SUBLANES, (q + 2) * SUBLANES)
            hr, hi = ar * hr - ai * hi + xs_ref[s, r0, re], ar * hi + ai * hr + xs_ref[s, r0, im]
            xs_ref[s, r0, re] = hr
            xs_ref[s, r0, im] = hi
        hst_ref[jb, :, re] = hr
        hst_ref[jb, :, im] = hi

        y4j = (jnp.dot(xs_ref[s, 0:m, re].astype(BF16), csr_ref[jb], preferred_element_type=F32)
               - jnp.dot(xs_ref[s, 0:m, im].astype(BF16), csi_ref[jb], preferred_element_type=F32)
               + jnp.dot(lhs, tmat_ref[jb], preferred_element_type=F32))
        for k in range(S5_TB):
            y4_ref[k * m:(k + 1) * m, lo:hi_] = y4j[:, k * S5_JCH:(k + 1) * S5_JCH]

    dskip = dskip_ref[...]
    bglu = bglu_ref[...]
    gpost = gpost_ref[...]
    for k in range(S5_TB):
        y = y4_ref[k * m:(k + 1) * m, :] + dskip * u4_ref[k * m:(k + 1) * m, :]
        y = jax.nn.gelu(y)
        z = jnp.dot(y.astype(BF16), wglu_ref[...], preferred_element_type=F32) + bglu
        out = x4_ref[k * m:(k + 1) * m, :] + _rms(y * jax.nn.sigmoid(z), gpost)
        for q in range(nq):
            r = q * blk + k * SUBLANES
            o_ref[r:r + SUBLANES, :] = out[q * SUBLANES:(q + 1) * SUBLANES, :]


def _const_spec(shape):
    nd = len(shape)
    return pl.BlockSpec(shape, lambda i, j: (0,) * nd, pipeline_mode=pl.Buffered(1))


def _row_spec(tm):
    return pl.BlockSpec((None, tm, D_MODEL), lambda i, j: (i, j, 0))


def _seq_spec(tm):
    return pl.BlockSpec((SUBLANES, tm // SUBLANES, D_MODEL), lambda i, j: (i, j, 0))


INTERLEAVED = jax.ShapeDtypeStruct((N_BLK, ROWS, D_MODEL), F32)
SEQ_MAJOR = jax.ShapeDtypeStruct((BATCH, SEQ, D_MODEL), F32)


def _s5_layer(xp, gpre, bin_, ab, csr, csi, tmat, dskip, wglu, bglu, gpost, *, x_seq_major):
    tm = TM_S5
    m = tm // S5_TB
    kern = functools.partial(_s5_kernel, tm=tm, x_seq_major=x_seq_major)
    consts = (gpre, bin_, ab, csr, csi, tmat, dskip, wglu, bglu, gpost)
    return pl.pallas_call(
        kern,
        grid=(N_BLK, ROWS // tm),
        in_specs=[_seq_spec(tm) if x_seq_major else _row_spec(tm)] + [_const_spec(c.shape) for c in consts],
        out_specs=_row_spec(tm),
        out_shape=INTERLEAVED,
        scratch_shapes=[
            pltpu.VMEM((tm, D_MODEL), F32),
            pltpu.VMEM((tm, D_MODEL), F32),
            pltpu.VMEM((tm, D_MODEL), BF16),
            pltpu.VMEM((2, SUBLANES + m, 2 * S5_JST), F32),
            pltpu.VMEM((tm, D_MODEL), F32),
            pltpu.VMEM((S5_NJ, SUBLANES, 2 * S5_JST), F32),
        ],
        compiler_params=pltpu.CompilerParams(
            dimension_semantics=("arbitrary", "arbitrary"),
            vmem_limit_bytes=VMEM_LIMIT),
        name="s5_mixer",
    )(xp, *consts)


def _pool_kernel(x_ref, gpre_ref, w_ref, scale_ref, gpost_ref, o_ref, buf_ref, m_ref, *, tm):
    j = pl.program_id(1)

    @pl.when(j == 0)
    def _():
        buf_ref[0:POOL_HALO, :] = jnp.zeros((POOL_HALO, D_MODEL), F32)

    @pl.when(j != 0)
    def _():
        buf_ref[0:POOL_HALO, :] = buf_ref[tm:tm + POOL_HALO, :]

    x = x_ref[...]
    u = _rms(x, gpre_ref[...])
    buf_ref[POOL_HALO:POOL_HALO + tm, :] = u

    row = lax.broadcasted_iota(jnp.int32, (tm, 1), 0)
    t1 = (j * (tm // SUBLANES) + row // SUBLANES + 1).astype(F32)

    for g, win in enumerate(POOL_WINDOWS):
        lo, hi = g * POOL_GROUP_CH, (g + 1) * POOL_GROUP_CH
        s = buf_ref[:, lo:hi]
        step = 1
        while step < win:
            sh = step * SUBLANES
            s = s[sh:, :] + s[:-sh, :]
            step *= 2
        s = s[s.shape[0] - tm:, :]
        cnt = jnp.minimum(t1, float(win))
        diff = s / cnt - u[:, lo:hi]
        m_ref[:, lo:hi] = jnp.dot(diff.astype(BF16), w_ref[g], preferred_element_type=F32)

    m = m_ref[...] * scale_ref[...]
    o_ref[...] = x + _rms(m, gpost_ref[...])


def _pool_layer(xp, gpre, w, scale, gpost):
    tm = TM_POOL
    kern = functools.partial(_pool_kernel, tm=tm)
    consts = (gpre, w, scale, gpost)
    return pl.pallas_call(
        kern,
        grid=(N_BLK, ROWS // tm),
        in_specs=[_row_spec(tm)] + [_const_spec(c.shape) for c in consts],
        out_specs=_row_spec(tm),
        out_shape=INTERLEAVED,
        scratch_shapes=[
            pltpu.VMEM((POOL_HALO + tm, D_MODEL), F32),
            pltpu.VMEM((tm, D_MODEL), F32),
        ],
        compiler_params=pltpu.CompilerParams(
            dimension_semantics=("arbitrary", "arbitrary"),
            vmem_limit_bytes=VMEM_LIMIT),
        name="pool_mixer",
    )(xp, *consts)


def _ffn_kernel(x_ref, gpre_ref, wg_ref, wv_ref, cw_ref, cb_ref, wd_ref, gpost_ref, o_ref,
                hb_ref, carry_ref, g_ref, v_ref, hdn_ref, *, tm, out_seq_major):
    j = pl.program_id(1)
    halo = (CONV_WIDTH - 1) * SUBLANES

    @pl.when(j == 0)
    def _():
        carry_ref[...] = jnp.zeros_like(carry_ref)

    x = x_ref[...]
    hb_ref[...] = _rms(x, gpre_ref[...]).astype(BF16)

    for c in range(FF_NCHUNK):
        s = c % 2
        g_ref[s, 0:halo, :] = carry_ref[c]
        cols = slice(c * FF_CHUNK, (c + 1) * FF_CHUNK)
        g_ref[s, halo:halo + tm, :] = jnp.dot(hb_ref[...], wg_ref[:, cols], preferred_element_type=F32)
        v_ref[s] = jnp.dot(hb_ref[...], wv_ref[:, cols], preferred_element_type=F32)
        carry_ref[c] = g_ref[s, tm:tm + halo, :]
        cw = cw_ref[c]
        gc = cb_ref[c] + cw[0:1, :] * g_ref[s, 0:tm, :]
        gc = gc + cw[1:2, :] * g_ref[s, SUBLANES:tm + SUBLANES, :]
        gc = gc + cw[2:3, :] * g_ref[s, halo:halo + tm, :]
        inner = gc * (GELU_C1 + GELU_C2 * (gc * gc))
        hdn = (gc * v_ref[s]) * (1.0 + jnp.tanh(inner))
        hdn_ref[:, cols] = hdn.astype(BF16)

    f = jnp.dot(hdn_ref[...], wd_ref[...], preferred_element_type=F32)
    out = x + _rms(f, gpost_ref[...])
    if out_seq_major:
        for t in range(tm // SUBLANES):
            o_ref[:, t, :] = out[t * SUBLANES:(t + 1) * SUBLANES, :]
    else:
        o_ref[...] = out


def _ffn_layer(xp, gpre, wg, wv, cw, cb, wd_half, gpost, *, out_seq_major):
    tm = TM_FFN
    kern = functools.partial(_ffn_kernel, tm=tm, out_seq_major=out_seq_major)
    halo = (CONV_WIDTH - 1) * SUBLANES
    consts = (gpre, wg, wv, cw, cb, wd_half, gpost)
    return pl.pallas_call(
        kern,
        grid=(N_BLK, ROWS // tm),
        in_specs=[_row_spec(tm)] + [_const_spec(c.shape) for c in consts],
        out_specs=_seq_spec(tm) if out_seq_major else _row_spec(tm),
        out_shape=SEQ_MAJOR if out_seq_major else INTERLEAVED,
        scratch_shapes=[
            pltpu.VMEM((tm, D_MODEL), BF16),
            pltpu.VMEM((FF_NCHUNK, halo, FF_CHUNK), F32),
            pltpu.VMEM((2, halo + tm, FF_CHUNK), F32),
            pltpu.VMEM((2, tm, FF_CHUNK), F32),
            pltpu.VMEM((tm, D_FF), BF16),
        ],
        compiler_params=pltpu.CompilerParams(
            dimension_semantics=("arbitrary", "arbitrary"),
            vmem_limit_bytes=VMEM_LIMIT),
        name="conv_ffn",
    )(xp, *consts)


def _chunk_cols(w):
    k = w.shape[0]
    return jnp.transpose(w.reshape(k, FF_NCHUNK, FF_CHUNK), (1, 0, 2))


def kernel(x, s5_lambda_re, s5_lambda_im, s5_log_dt, s5_b_re, s5_b_im, s5_c_re, s5_c_im, s5_d, s5_w_glu, s5_b_glu, pool_w, pool_scale, ffn_w_gate, ffn_w_val, ffn_conv_w, ffn_conv_b, ffn_w_down, norm_mix_pre, norm_mix_post, norm_ffn_pre, norm_ffn_post):
    row = lambda v: v.reshape(1, D_MODEL)
    xp = x

    for i in range(DEPTH):
        jl = i // 2
        if i % 2 == 0:
            prep = _s5_prep(s5_lambda_re[jl], s5_lambda_im[jl], s5_log_dt[jl],
                            s5_b_re[jl], s5_b_im[jl], s5_c_re[jl], s5_c_im[jl])
            bin_, ab, csr, csi, tmat = _s5_operators(*prep)
            xp = _s5_layer(
                xp, row(norm_mix_pre[i]), bin_, ab, csr, csi, tmat, row(s5_d[jl]),
                s5_w_glu[jl].astype(BF16), row(s5_b_glu[jl]), row(norm_mix_post[i]),
                x_seq_major=(i == 0))
        else:
            xp = _pool_layer(xp, row(norm_mix_pre[i]), pool_w[jl].astype(BF16),
                             row(pool_scale[jl]), row(norm_mix_post[i]))
        xp = _ffn_layer(
            xp, row(norm_ffn_pre[i]),
            ffn_w_gate[i].astype(BF16), ffn_w_val[i].astype(BF16),
            _chunk_cols(ffn_conv_w[i]), ffn_conv_b[i].reshape(FF_NCHUNK, 1, FF_CHUNK),
            (0.5 * ffn_w_down[i]).astype(BF16),
            row(norm_ffn_post[i]), out_seq_major=(i == DEPTH - 1))

    return xp
```

```python
import functools
import math

import jax
import jax.numpy as jnp
from jax import lax
from jax.experimental import pallas as pl
from jax.experimental.pallas import tpu as pltpu

D_MODEL = 1024
BATCH = 32
SEQ = 2048
DEPTH = 4
S5_GROUP_CH = 16
S5_GROUPS = D_MODEL // S5_GROUP_CH
S5_STATE = 64
POOL_WINDOWS = (2, 4, 8, 16)
POOL_GROUP_CH = D_MODEL // len(POOL_WINDOWS)
D_FF = 2816
CONV_WIDTH = 3
RMS_EPS = 1e-6

SUBLANES = 8
LANES = 128

N_BLK = BATCH // SUBLANES
ROWS = SEQ * SUBLANES
S5_JBLK = LANES // S5_GROUP_CH
S5_NJ = S5_GROUPS // S5_JBLK
S5_JCH = S5_JBLK * S5_GROUP_CH
S5_JST = S5_JBLK * S5_STATE
S5_TB = 2
FF_CHUNK = 256
FF_NCHUNK = D_FF // FF_CHUNK
POOL_HALO = (max(POOL_WINDOWS) - 1) * SUBLANES + SUBLANES

TM_S5 = 1024
TM_POOL = 1024
TM_FFN = 1024
VMEM_LIMIT = 60 * 1024 * 1024

F32 = jnp.float32
BF16 = jnp.bfloat16

GELU_C1 = math.sqrt(2.0 / math.pi)
GELU_C2 = GELU_C1 * 0.044715


def _rms(x, gain):
    return x * lax.rsqrt(jnp.mean(x * x, axis=-1, keepdims=True) + RMS_EPS) * gain


def _cmul(ar, ai, br, bi):
    return ar * br - ai * bi, ar * bi + ai * br


def _s5_prep_kernel(lr_ref, li_ref, ldt_ref, br_ref, bi_ref, cr_ref, ci_ref,
                    ab_ref, bin_ref, csr_ref, csi_ref, tmat_ref):
    lr = lr_ref[...]
    li = li_ref[...]
    dt = jnp.exp(ldt_ref[...])
    mag = jnp.exp(lr * dt)
    ab_re = mag * jnp.cos(li * dt)
    ab_im = mag * jnp.sin(li * dt)
    den = lr * lr + li * li
    nr = ab_re - 1.0
    ni = ab_im
    f_re = (nr * lr + ni * li) / den
    f_im = (ni * lr - nr * li) / den
    over_h = lambda v: v[:, None, :]
    bbr, bbi = _cmul(over_h(f_re), over_h(f_im), br_ref[...], bi_ref[...])

    pw = [None, (ab_re, ab_im)]
    for _ in range(S5_TB - 1):
        pw.append(_cmul(pw[-1][0], pw[-1][1], ab_re, ab_im))
    ab_ref[0] = pw[S5_TB][0]
    ab_ref[1] = pw[S5_TB][1]

    G, P, H = S5_GROUPS, S5_STATE, S5_GROUP_CH
    iota = lambda shape, dim: lax.broadcasted_iota(jnp.int32, shape, dim)
    flat = lambda v: v.reshape(G * H, P)

    spread_p = (iota((P, S5_JST), 1) % P == iota((P, S5_JST), 0)).astype(BF16)
    own_in = (iota((G * H, S5_JST), 0) // H) % S5_JBLK == iota((G * H, S5_JST), 1) // P

    for k in range(S5_TB):
        d = S5_TB - 1 - k
        piece = (bbr, bbi) if d == 0 else _cmul(over_h(pw[d][0]), over_h(pw[d][1]), bbr, bbi)
        for c in range(2):
            e = jnp.dot(flat(piece[c]).astype(BF16), spread_p, preferred_element_type=F32)
            e = jnp.where(own_in, e, 0.0).astype(BF16).reshape(S5_NJ, S5_JCH, S5_JST)
            bin_ref[:, k * S5_JCH:(k + 1) * S5_JCH, c * S5_JST:(c + 1) * S5_JST] = e

    cr = cr_ref[...]
    ci = ci_ref[...]
    ca = [(cr, ci)]
    for d in range(1, S5_TB + 1):
        ca.append(_cmul(over_h(pw[d][0]), over_h(pw[d][1]), cr, ci))

    nt = (((1,), (1,)), ((), ()))
    spread_rows = (iota((S5_JST, P), 0) % P == iota((S5_JST, P), 1)).astype(BF16)
    own_out = iota((S5_JST, S5_JCH), 0) // P == iota((S5_JST, S5_JCH), 1) // H
    own_t = iota((S5_JCH, S5_JCH), 0) // H == iota((S5_JCH, S5_JCH), 1) // H
    bb_flat = (flat(bbr), flat(bbi))
    ca_flat = [(flat(r), flat(i)) for r, i in ca]
    cs_refs = (csr_ref, csi_ref)
    for j in range(S5_NJ):
        rows = slice(j * S5_JCH, (j + 1) * S5_JCH)
        for k in range(S5_TB):
            for c in range(2):
                e = lax.dot_general(spread_rows, ca_flat[k + 1][c][rows].astype(BF16), nt,
                                    preferred_element_type=F32)
                cs_refs[c][j, :, k * S5_JCH:(k + 1) * S5_JCH] = jnp.where(own_out, e, 0.0).astype(BF16)
        hi = lax.Precision.HIGHEST
        for d in range(S5_TB):
            t = (lax.dot_general(bb_flat[0][rows], ca_flat[d][0][rows], nt, precision=hi, preferred_element_type=F32)
                 - lax.dot_general(bb_flat[1][rows], ca_flat[d][1][rows], nt, precision=hi, preferred_element_type=F32))
            t = jnp.where(own_t, t, 0.0).astype(BF16)
            for m in range(S5_TB - d):
                k = m + d
                tmat_ref[j, m * S5_JCH:(m + 1) * S5_JCH, k * S5_JCH:(k + 1) * S5_JCH] = t
        for m in range(S5_TB):
            for k in range(m):
                tmat_ref[j, m * S5_JCH:(m + 1) * S5_JCH, k * S5_JCH:(k + 1) * S5_JCH] = jnp.zeros(
                    (S5_JCH, S5_JCH), BF16)


def _s5_prep(lam_re, lam_im, log_dt, b_re, b_im, c_re, c_im):
    G, P = S5_GROUPS, S5_STATE
    br_t = jnp.transpose(b_re, (0, 2, 1))
    bi_t = jnp.transpose(b_im, (0, 2, 1))
    tch = S5_TB * S5_JCH
    ab, bin_, csr, csi, tmat = pl.pallas_call(
        _s5_prep_kernel,
        out_shape=(
            jax.ShapeDtypeStruct((2, G, P), F32),
            jax.ShapeDtypeStruct((S5_NJ, tch, 2 * S5_JST), BF16),
            jax.ShapeDtypeStruct((S5_NJ, S5_JST, tch), BF16),
            jax.ShapeDtypeStruct((S5_NJ, S5_JST, tch), BF16),
            jax.ShapeDtypeStruct((S5_NJ, tch, tch), BF16),
        ),
        name="s5_prep",
    )(lam_re, lam_im, log_dt.reshape(G, 1), br_t, bi_t, c_re, c_im)
    return bin_, ab.reshape(2, S5_NJ, 1, S5_JST), csr, csi, tmat


def _s5_kernel(x_ref, gpre_ref, bin_ref, ab_ref, csr_ref, csi_ref, tmat_ref, dskip_ref,
               wglu_ref, bglu_ref, gpost_ref, o_ref,
               x4_ref, u4_ref, ub4_ref, xs_ref, y4_ref, hst_ref, *, tm, x_seq_major):
    j = pl.program_id(1)
    m = tm // S5_TB
    nq = m // SUBLANES
    blk = S5_TB * SUBLANES

    @pl.when(j == 0)
    def _():
        hst_ref[...] = jnp.zeros_like(hst_ref)

    gpre = gpre_ref[...]
    def x_step(t):
        if x_seq_major:
            return x_ref[:, t, :]
        return x_ref[t * SUBLANES:(t + 1) * SUBLANES, :]

    for k in range(S5_TB):
        xk = jnp.concatenate([x_step(q * S5_TB + k) for q in range(nq)], axis=0)
        x4_ref[k * m:(k + 1) * m, :] = xk
        uk = _rms(xk, gpre)
        u4_ref[k * m:(k + 1) * m, :] = uk
        ub4_ref[k * m:(k + 1) * m, :] = uk.astype(BF16)

    def block_inputs(jb):
        lo = jb * S5_JCH
        return jnp.concatenate([ub4_ref[k * m:(k + 1) * m, lo:lo + S5_JCH] for k in range(S5_TB)], axis=1)

    def in_proj(jb):
        xs_ref[jb % 2, SUBLANES:SUBLANES + m, :] = jnp.dot(block_inputs(jb), bin_ref[jb],
                                                             preferred_element_type=F32)

    in_proj(0)
    for jb in range(S5_NJ):
        s = jb % 2
        lo, hi_ = jb * S5_JCH, (jb + 1) * S5_JCH
        if jb + 1 < S5_NJ:
            in_proj(jb + 1)
        lhs = block_inputs(jb)
        ar, ai = (jnp.broadcast_to(ab_ref[n, jb], (SUBLANES, S5_JST)) for n in range(2))
        re, im = slice(0, S5_JST), slice(S5_JST, 2 * S5_JST)
        hr = hst_ref[jb, :, re]
        hi = hst_ref[jb, :, im]
        xs_ref[s, 0:SUBLANES, re] = hr
        xs_ref[s, 0:SUBLANES, im] = hi
        for q in range(nq):
            r0 = slice((q + 1) * SUBLANES, (q + 2) * SUBLANES)
            hr, hi = ar * hr - ai * hi + xs_ref[s, r0, re], ar * hi + ai * hr + xs_ref[s, r0, im]
            xs_ref[s, r0, re] = hr
            xs_ref[s, r0, im] = hi
        hst_ref[jb, :, re] = hr
        hst_ref[jb, :, im] = hi

        y4j = (jnp.dot(xs_ref[s, 0:m, re].astype(BF16), csr_ref[jb], preferred_element_type=F32)
               - jnp.dot(xs_ref[s, 0:m, im].astype(BF16), csi_ref[jb], preferred_element_type=F32)
               + jnp.dot(lhs, tmat_ref[jb], preferred_element_type=F32))
        for k in range(S5_TB):
            y4_ref[k * m:(k + 1) * m, lo:hi_] = y4j[:, k * S5_JCH:(k + 1) * S5_JCH]

    dskip = dskip_ref[...]
    bglu = bglu_ref[...]
    gpost = gpost_ref[...]
    for k in range(S5_TB):
        y = y4_ref[k * m:(k + 1) * m, :] + dskip * u4_ref[k * m:(k + 1) * m, :]
        y = jax.nn.gelu(y)
        z = jnp.dot(y.astype(BF16), wglu_ref[...], preferred_element_type=F32) + bglu
        out = x4_ref[k * m:(k + 1) * m, :] + _rms(y * jax.nn.sigmoid(z), gpost)
        for q in range(nq):
            r = q * blk + k * SUBLANES
            o_ref[r:r + SUBLANES, :] = out[q * SUBLANES:(q + 1) * SUBLANES, :]


def _const_spec(shape):
    nd = len(shape)
    return pl.BlockSpec(shape, lambda i, j: (0,) * nd, pipeline_mode=pl.Buffered(1))


def _row_spec(tm):
    return pl.BlockSpec((None, tm, D_MODEL), lambda i, j: (i, j, 0))


def _seq_spec(tm):
    return pl.BlockSpec((SUBLANES, tm // SUBLANES, D_MODEL), lambda i, j: (i, j, 0))


INTERLEAVED = jax.ShapeDtypeStruct((N_BLK, ROWS, D_MODEL), F32)
SEQ_MAJOR = jax.ShapeDtypeStruct((BATCH, SEQ, D_MODEL), F32)


def _s5_layer(xp, gpre, bin_, ab, csr, csi, tmat, dskip, wglu, bglu, gpost, *, x_seq_major):
    tm = TM_S5
    m = tm // S5_TB
    kern = functools.partial(_s5_kernel, tm=tm, x_seq_major=x_seq_major)
    consts = (gpre, bin_, ab, csr, csi, tmat, dskip, wglu, bglu, gpost)
    return pl.pallas_call(
        kern,
        grid=(N_BLK, ROWS // tm),
        in_specs=[_seq_spec(tm) if x_seq_major else _row_spec(tm)] + [_const_spec(c.shape) for c in consts],
        out_specs=_row_spec(tm),
        out_shape=INTERLEAVED,
        scratch_shapes=[
            pltpu.VMEM((tm, D_MODEL), F32),
            pltpu.VMEM((tm, D_MODEL), F32),
            pltpu.VMEM((tm, D_MODEL), BF16),
            pltpu.VMEM((2, SUBLANES + m, 2 * S5_JST), F32),
            pltpu.VMEM((tm, D_MODEL), F32),
            pltpu.VMEM((S5_NJ, SUBLANES, 2 * S5_JST), F32),
        ],
        compiler_params=pltpu.CompilerParams(
            dimension_semantics=("arbitrary", "arbitrary"),
            vmem_limit_bytes=VMEM_LIMIT),
        name="s5_mixer",
    )(xp, *consts)


def _pool_kernel(x_ref, gpre_ref, w_ref, scale_ref, gpost_ref, o_ref, buf_ref, m_ref, *, tm):
    j = pl.program_id(1)

    @pl.when(j == 0)
    def _():
        buf_ref[0:POOL_HALO, :] = jnp.zeros((POOL_HALO, D_MODEL), F32)

    @pl.when(j != 0)
    def _():
        buf_ref[0:POOL_HALO, :] = buf_ref[tm:tm + POOL_HALO, :]

    x = x_ref[...]
    u = _rms(x, gpre_ref[...])
    buf_ref[POOL_HALO:POOL_HALO + tm, :] = u

    row = lax.broadcasted_iota(jnp.int32, (tm, 1), 0)
    t1 = (j * (tm // SUBLANES) + row // SUBLANES + 1).astype(F32)

    for g, win in enumerate(POOL_WINDOWS):
        lo, hi = g * POOL_GROUP_CH, (g + 1) * POOL_GROUP_CH
        s = buf_ref[:, lo:hi]
        step = 1
        while step < win:
            sh = step * SUBLANES
            s = s[sh:, :] + s[:-sh, :]
            step *= 2
        s = s[s.shape[0] - tm:, :]
        cnt = jnp.minimum(t1, float(win))
        diff = s / cnt - u[:, lo:hi]
        m_ref[:, lo:hi] = jnp.dot(diff.astype(BF16), w_ref[g], preferred_element_type=F32)

    m = m_ref[...] * scale_ref[...]
    o_ref[...] = x + _rms(m, gpost_ref[...])


def _pool_layer(xp, gpre, w, scale, gpost):
    tm = TM_POOL
    kern = functools.partial(_pool_kernel, tm=tm)
    consts = (gpre, w, scale, gpost)
    return pl.pallas_call(
        kern,
        grid=(N_BLK, ROWS // tm),
        in_specs=[_row_spec(tm)] + [_const_spec(c.shape) for c in consts],
        out_specs=_row_spec(tm),
        out_shape=INTERLEAVED,
        scratch_shapes=[
            pltpu.VMEM((POOL_HALO + tm, D_MODEL), F32),
            pltpu.VMEM((tm, D_MODEL), F32),
        ],
        compiler_params=pltpu.CompilerParams(
            dimension_semantics=("arbitrary", "arbitrary"),
            vmem_limit_bytes=VMEM_LIMIT),
        name="pool_mixer",
    )(xp, *consts)


def _ffn_kernel(x_ref, gpre_ref, wg_ref, wv_ref, cw_ref, cb_ref, wd_ref, gpost_ref, o_ref,
                hb_ref, carry_ref, g_ref, v_ref, hdn_ref, *, tm, out_seq_major):
    j = pl.program_id(1)
    halo = (CONV_WIDTH - 1) * SUBLANES

    @pl.when(j == 0)
    def _():
        carry_ref[...] = jnp.zeros_like(carry_ref)

    x = x_ref[...]
    hb_ref[...] = _rms(x, gpre_ref[...]).astype(BF16)

    for c in range(FF_NCHUNK):
        s = c % 2
        g_ref[s, 0:halo, :] = carry_ref[c]
        cols = slice(c * FF_CHUNK, (c + 1) * FF_CHUNK)
        g_ref[s, halo:halo + tm, :] = jnp.dot(hb_ref[...], wg_ref[:, cols], preferred_element_type=F32)
        v_ref[s] = jnp.dot(hb_ref[...], wv_ref[:, cols], preferred_element_type=F32)
        carry_ref[c] = g_ref[s, tm:tm + halo, :]
        cw = cw_ref[c]
        gc = cb_ref[c] + cw[0:1, :] * g_ref[s, 0:tm, :]
        gc = gc + cw[1:2, :] * g_ref[s, SUBLANES:tm + SUBLANES, :]
        gc = gc + cw[2:3, :] * g_ref[s, halo:halo + tm, :]
        inner = gc * (GELU_C1 + GELU_C2 * (gc * gc))
        hdn = (gc * v_ref[s]) * (1.0 + jnp.tanh(inner))
        hdn_ref[:, cols] = hdn.astype(BF16)

    f = jnp.dot(hdn_ref[...], wd_ref[...], preferred_element_type=F32)
    out = x + _rms(f, gpost_ref[...])
    if out_seq_major:
        for t in range(tm // SUBLANES):
            o_ref[:, t, :] = out[t * SUBLANES:(t + 1) * SUBLANES, :]
    else:
        o_ref[...] = out


def _ffn_layer(xp, gpre, wg, wv, cw, cb, wd_half, gpost, *, out_seq_major):
    tm = TM_FFN
    kern = functools.partial(_ffn_kernel, tm=tm, out_seq_major=out_seq_major)
    halo = (CONV_WIDTH - 1) * SUBLANES
    consts = (gpre, wg, wv, cw, cb, wd_half, gpost)
    return pl.pallas_call(
        kern,
        grid=(N_BLK, ROWS // tm),
        in_specs=[_row_spec(tm)] + [_const_spec(c.shape) for c in consts],
        out_specs=_seq_spec(tm) if out_seq_major else _row_spec(tm),
        out_shape=SEQ_MAJOR if out_seq_major else INTERLEAVED,
        scratch_shapes=[
            pltpu.VMEM((tm, D_MODEL), BF16),
            pltpu.VMEM((FF_NCHUNK, halo, FF_CHUNK), F32),
            pltpu.VMEM((2, halo + tm, FF_CHUNK), F32),
            pltpu.VMEM((2, tm, FF_CHUNK), F32),
            pltpu.VMEM((tm, D_FF), BF16),
        ],
        compiler_params=pltpu.CompilerParams(
            dimension_semantics=("arbitrary", "arbitrary"),
            vmem_limit_bytes=VMEM_LIMIT),
        name="conv_ffn",
    )(xp, *consts)


def _chunk_cols(w):
    k = w.shape[0]
    return jnp.transpose(w.reshape(k, FF_NCHUNK, FF_CHUNK), (1, 0, 2))


def kernel(x, s5_lambda_re, s5_lambda_im, s5_log_dt, s5_b_re, s5_b_im, s5_c_re, s5_c_im, s5_d, s5_w_glu, s5_b_glu, pool_w, pool_scale, ffn_w_gate, ffn_w_val, ffn_conv_w, ffn_conv_b, ffn_w_down, norm_mix_pre, norm_mix_post, norm_ffn_pre, norm_ffn_post):
    row = lambda v: v.reshape(1, D_MODEL)
    xp = x

    for i in range(DEPTH):
        jl = i // 2
        if i % 2 == 0:
            bin_, ab, csr, csi, tmat = _s5_prep(s5_lambda_re[jl], s5_lambda_im[jl], s5_log_dt[jl],
                                                s5_b_re[jl], s5_b_im[jl], s5_c_re[jl], s5_c_im[jl])
            xp = _s5_layer(
                xp, row(norm_mix_pre[i]), bin_, ab, csr, csi, tmat, row(s5_d[jl]),
                s5_w_glu[jl].astype(BF16), row(s5_b_glu[jl]), row(norm_mix_post[i]),
                x_seq_major=(i == 0))
        else:
            xp = _pool_layer(xp, row(norm_mix_pre[i]), pool_w[jl].astype(BF16),
                             row(pool_scale[jl]), row(norm_mix_post[i]))
        xp = _ffn_layer(
            xp, row(norm_ffn_pre[i]),
            ffn_w_gate[i].astype(BF16), ffn_w_val[i].astype(BF16),
            _chunk_cols(ffn_conv_w[i]), ffn_conv_b[i].reshape(FF_NCHUNK, 1, FF_CHUNK),
            (0.5 * ffn_w_down[i]).astype(BF16),
            row(norm_ffn_post[i]), out_seq_major=(i == DEPTH - 1))

    return xp
```

```python
import functools
import math

import jax
import jax.numpy as jnp
from jax import lax
from jax.experimental import pallas as pl
from jax.experimental.pallas import tpu as pltpu

D_MODEL = 1024
BATCH = 32
SEQ = 2048
DEPTH = 4
S5_GROUP_CH = 16
S5_GROUPS = D_MODEL // S5_GROUP_CH
S5_STATE = 64
POOL_WINDOWS = (2, 4, 8, 16)
POOL_GROUP_CH = D_MODEL // len(POOL_WINDOWS)
D_FF = 2816
CONV_WIDTH = 3
RMS_EPS = 1e-6

SUBLANES = 8
LANES = 128

N_BLK = BATCH // SUBLANES
ROWS = SEQ * SUBLANES
S5_JBLK = LANES // S5_GROUP_CH
S5_NJ = S5_GROUPS // S5_JBLK
S5_JCH = S5_JBLK * S5_GROUP_CH
S5_JST = S5_JBLK * S5_STATE
S5_TB = 2
FF_CHUNK = 256
FF_NCHUNK = D_FF // FF_CHUNK
POOL_HALO = (max(POOL_WINDOWS) - 1) * SUBLANES + SUBLANES

TM_S5 = 1024
TM_POOL = 1024
TM_FFN = 1024
VMEM_LIMIT = 60 * 1024 * 1024

F32 = jnp.float32
BF16 = jnp.bfloat16

GELU_C1 = math.sqrt(2.0 / math.pi)
GELU_C2 = GELU_C1 * 0.044715


def _rms(x, gain):
    return x * lax.rsqrt(jnp.mean(x * x, axis=-1, keepdims=True) + RMS_EPS) * gain


def _cmul(ar, ai, br, bi):
    return ar * br - ai * bi, ar * bi + ai * br


def _s5_prep_kernel(lr_ref, li_ref, ldt_ref, br_ref, bi_ref, cr_ref, ci_ref,
                    ab_ref, bin_ref, wout_ref):
    lr = lr_ref[...]
    li = li_ref[...]
    dt = jnp.exp(ldt_ref[...])
    mag = jnp.exp(lr * dt)
    ab_re = mag * jnp.cos(li * dt)
    ab_im = mag * jnp.sin(li * dt)
    den = lr * lr + li * li
    nr = ab_re - 1.0
    ni = ab_im
    f_re = (nr * lr + ni * li) / den
    f_im = (ni * lr - nr * li) / den
    over_h = lambda v: v[:, None, :]
    bbr, bbi = _cmul(over_h(f_re), over_h(f_im), br_ref[...], bi_ref[...])

    pw = [None, (ab_re, ab_im)]
    for _ in range(S5_TB - 1):
        pw.append(_cmul(pw[-1][0], pw[-1][1], ab_re, ab_im))
    ab_ref[0] = pw[S5_TB][0]
    ab_ref[1] = pw[S5_TB][1]

    G, P, H = S5_GROUPS, S5_STATE, S5_GROUP_CH
    iota = lambda shape, dim: lax.broadcasted_iota(jnp.int32, shape, dim)
    flat = lambda v: v.reshape(G * H, P)

    spread_p = (iota((P, S5_JST), 1) % P == iota((P, S5_JST), 0)).astype(BF16)
    own_in = (iota((G * H, S5_JST), 0) // H) % S5_JBLK == iota((G * H, S5_JST), 1) // P

    for k in range(S5_TB):
        d = S5_TB - 1 - k
        piece = (bbr, bbi) if d == 0 else _cmul(over_h(pw[d][0]), over_h(pw[d][1]), bbr, bbi)
        for c in range(2):
            e = jnp.dot(flat(piece[c]).astype(BF16), spread_p, preferred_element_type=F32)
            e = jnp.where(own_in, e, 0.0).astype(BF16).reshape(S5_NJ, S5_JCH, S5_JST)
            bin_ref[:, k * S5_JCH:(k + 1) * S5_JCH, c * S5_JST:(c + 1) * S5_JST] = e

    cr = cr_ref[...]
    ci = ci_ref[...]
    ca = [(cr, ci)]
    for d in range(1, S5_TB + 1):
        ca.append(_cmul(over_h(pw[d][0]), over_h(pw[d][1]), cr, ci))

    nt = (((1,), (1,)), ((), ()))
    spread_rows = (iota((S5_JST, P), 0) % P == iota((S5_JST, P), 1)).astype(BF16)
    own_out = iota((S5_JST, S5_JCH), 0) // P == iota((S5_JST, S5_JCH), 1) // H
    own_t = iota((S5_JCH, S5_JCH), 0) // H == iota((S5_JCH, S5_JCH), 1) // H
    bb_flat = (flat(bbr), flat(bbi))
    ca_flat = [(flat(r), flat(i)) for r, i in ca]
    tch = S5_TB * S5_JCH
    for j in range(S5_NJ):
        rows = slice(j * S5_JCH, (j + 1) * S5_JCH)
        chunk = lambda k: slice(k * S5_JCH, (k + 1) * S5_JCH)
        hi = lax.Precision.HIGHEST
        for d in range(S5_TB):
            t = (lax.dot_general(bb_flat[0][rows], ca_flat[d][0][rows], nt, precision=hi, preferred_element_type=F32)
                 - lax.dot_general(bb_flat[1][rows], ca_flat[d][1][rows], nt, precision=hi, preferred_element_type=F32))
            t = jnp.where(own_t, t, 0.0).astype(BF16)
            for m in range(S5_TB - d):
                wout_ref[j, chunk(m), chunk(m + d)] = t
        for m in range(S5_TB):
            for k in range(m):
                wout_ref[j, chunk(m), chunk(k)] = jnp.zeros((S5_JCH, S5_JCH), BF16)
        for k in range(S5_TB):
            for c in range(2):
                e = lax.dot_general(spread_rows, ca_flat[k + 1][c][rows].astype(BF16), nt,
                                    preferred_element_type=F32)
                e = jnp.where(own_out, e if c == 0 else -e, 0.0).astype(BF16)
                wout_ref[j, tch + c * S5_JST:tch + (c + 1) * S5_JST, chunk(k)] = e


def _s5_prep(lam_re, lam_im, log_dt, b_re, b_im, c_re, c_im):
    G, P = S5_GROUPS, S5_STATE
    br_t = jnp.transpose(b_re, (0, 2, 1))
    bi_t = jnp.transpose(b_im, (0, 2, 1))
    tch = S5_TB * S5_JCH
    ab, bin_, wout = pl.pallas_call(
        _s5_prep_kernel,
        out_shape=(
            jax.ShapeDtypeStruct((2, G, P), F32),
            jax.ShapeDtypeStruct((S5_NJ, tch, 2 * S5_JST), BF16),
            jax.ShapeDtypeStruct((S5_NJ, tch + 2 * S5_JST, tch), BF16),
        ),
        name="s5_prep",
    )(lam_re, lam_im, log_dt.reshape(G, 1), br_t, bi_t, c_re, c_im)
    return bin_, ab.reshape(2, S5_NJ, 1, S5_JST), wout


def _s5_kernel(x_ref, gpre_ref, bin_ref, ab_ref, wout_ref, dskip_ref,
               wglu_ref, bglu_ref, gpost_ref, o_ref,
               x4_ref, u4_ref, ub4_ref, xs_ref, y4_ref, hst_ref, *, tm, x_seq_major):
    j = pl.program_id(1)
    m = tm // S5_TB
    nq = m // SUBLANES
    blk = S5_TB * SUBLANES

    @pl.when(j == 0)
    def _():
        hst_ref[...] = jnp.zeros_like(hst_ref)

    gpre = gpre_ref[...]
    def x_step(t):
        if x_seq_major:
            return x_ref[:, t, :]
        return x_ref[t * SUBLANES:(t + 1) * SUBLANES, :]

    for k in range(S5_TB):
        xk = jnp.concatenate([x_step(q * S5_TB + k) for q in range(nq)], axis=0)
        x4_ref[k * m:(k + 1) * m, :] = xk
        uk = _rms(xk, gpre)
        u4_ref[k * m:(k + 1) * m, :] = uk
        ub4_ref[k * m:(k + 1) * m, :] = uk.astype(BF16)

    def block_inputs(jb):
        lo = jb * S5_JCH
        return jnp.concatenate([ub4_ref[k * m:(k + 1) * m, lo:lo + S5_JCH] for k in range(S5_TB)], axis=1)

    def in_proj(jb):
        xs_ref[jb % 2, SUBLANES:SUBLANES + m, :] = jnp.dot(block_inputs(jb), bin_ref[jb],
                                                             preferred_element_type=F32)

    in_proj(0)
    for jb in range(S5_NJ):
        s = jb % 2
        lo, hi_ = jb * S5_JCH, (jb + 1) * S5_JCH
        if jb + 1 < S5_NJ:
            in_proj(jb + 1)
        lhs = block_inputs(jb)
        ar, ai = (jnp.broadcast_to(ab_ref[n, jb], (SUBLANES, S5_JST)) for n in range(2))
        re, im = slice(0, S5_JST), slice(S5_JST, 2 * S5_JST)
        hr = hst_ref[jb, :, re]
        hi = hst_ref[jb, :, im]
        xs_ref[s, 0:SUBLANES, re] = hr
        xs_ref[s, 0:SUBLANES, im] = hi
        for q in range(nq):
            r0 = slice((q + 1) * SUBLANES, (q + 2) * SUBLANES)
            hr, hi = ar * hr - ai * hi + xs_ref[s, r0, re], ar * hi + ai * hr + xs_ref[s, r0, im]
            xs_ref[s, r0, re] = hr
            xs_ref[s, r0, im] = hi
        hst_ref[jb, :, re] = hr
        hst_ref[jb, :, im] = hi

        lhs_all = jnp.concatenate(
            [lhs, xs_ref[s, 0:m, re].astype(BF16), xs_ref[s, 0:m, im].astype(BF16)], axis=1)
        y4j = jnp.dot(lhs_all, wout_ref[jb], preferred_element_type=F32)
        for k in range(S5_TB):
            y4_ref[k * m:(k + 1) * m, lo:hi_] = y4j[:, k * S5_JCH:(k + 1) * S5_JCH]

    dskip = dskip_ref[...]
    bglu = bglu_ref[...]
    gpost = gpost_ref[...]
    for k in range(S5_TB):
        y = y4_ref[k * m:(k + 1) * m, :] + dskip * u4_ref[k * m:(k + 1) * m, :]
        y = jax.nn.gelu(y)
        z = jnp.dot(y.astype(BF16), wglu_ref[...], preferred_element_type=F32) + bglu
        out = x4_ref[k * m:(k + 1) * m, :] + _rms(y * jax.nn.sigmoid(z), gpost)
        for q in range(nq):
            r = q * blk + k * SUBLANES
            o_ref[r:r + SUBLANES, :] = out[q * SUBLANES:(q + 1) * SUBLANES, :]


def _const_spec(shape):
    nd = len(shape)
    return pl.BlockSpec(shape, lambda i, j: (0,) * nd, pipeline_mode=pl.Buffered(1))


def _row_spec(tm):
    return pl.BlockSpec((None, tm, D_MODEL), lambda i, j: (i, j, 0))


def _seq_spec(tm):
    return pl.BlockSpec((SUBLANES, tm // SUBLANES, D_MODEL), lambda i, j: (i, j, 0))


INTERLEAVED = jax.ShapeDtypeStruct((N_BLK, ROWS, D_MODEL), F32)
SEQ_MAJOR = jax.ShapeDtypeStruct((BATCH, SEQ, D_MODEL), F32)


def _s5_layer(xp, gpre, bin_, ab, wout, dskip, wglu, bglu, gpost, *, x_seq_major):
    tm = TM_S5
    m = tm // S5_TB
    kern = functools.partial(_s5_kernel, tm=tm, x_seq_major=x_seq_major)
    consts = (gpre, bin_, ab, wout, dskip, wglu, bglu, gpost)
    return pl.pallas_call(
        kern,
        grid=(N_BLK, ROWS // tm),
        in_specs=[_seq_spec(tm) if x_seq_major else _row_spec(tm)] + [_const_spec(c.shape) for c in consts],
        out_specs=_row_spec(tm),
        out_shape=INTERLEAVED,
        scratch_shapes=[
            pltpu.VMEM((tm, D_MODEL), F32),
            pltpu.VMEM((tm, D_MODEL), F32),
            pltpu.VMEM((tm, D_MODEL), BF16),
            pltpu.VMEM((2, SUBLANES + m, 2 * S5_JST), F32),
            pltpu.VMEM((tm, D_MODEL), F32),
            pltpu.VMEM((S5_NJ, SUBLANES, 2 * S5_JST), F32),
        ],
        compiler_params=pltpu.CompilerParams(
            dimension_semantics=("arbitrary", "arbitrary"),
            vmem_limit_bytes=VMEM_LIMIT),
        name="s5_mixer",
    )(xp, *consts)


POOL_RING = 3


def _pool_kernel(x_hbm, w_ref, gpost_ref, o_ref, xring, sems, buf_ref, m_ref, *, tm):
    i = pl.program_id(0)
    j = pl.program_id(1)
    nj = pl.num_programs(1)
    step = i * nj + j
    n_steps = pl.num_programs(0) * nj

    def tile_copy(t):
        slot = lax.rem(t, POOL_RING)
        return pltpu.make_async_copy(x_hbm.at[t // nj, pl.ds((t % nj) * tm, tm), :], xring.at[slot], sems.at[slot])

    @pl.when(step == 0)
    def _():
        for t in range(POOL_RING - 1):
            tile_copy(t).start()

    @pl.when(step + (POOL_RING - 1) < n_steps)
    def _():
        tile_copy(step + (POOL_RING - 1)).start()

    tile_copy(step).wait()
    x_ref = xring.at[lax.rem(step, POOL_RING)]

    @pl.when(j == 0)
    def _():
        buf_ref[0:POOL_HALO, :] = jnp.zeros((POOL_HALO, D_MODEL), F32)

    @pl.when(j != 0)
    def _():
        buf_ref[0:POOL_HALO, :] = buf_ref[tm:tm + POOL_HALO, :]

    x = x_ref[...]
    u = x * lax.rsqrt(jnp.mean(x * x, axis=-1, keepdims=True) + RMS_EPS)
    buf_ref[POOL_HALO:POOL_HALO + tm, :] = u

    row = lax.broadcasted_iota(jnp.int32, (tm, 1), 0)
    t1 = (j * (tm // SUBLANES) + row // SUBLANES + 1).astype(F32)

    for g, win in enumerate(POOL_WINDOWS):
        lo, hi = g * POOL_GROUP_CH, (g + 1) * POOL_GROUP_CH
        s = buf_ref[:, lo:hi]
        span = 1
        while span < win:
            sh = span * SUBLANES
            s = s[sh:, :] + s[:-sh, :]
            span *= 2
        s = s[s.shape[0] - tm:, :]
        cnt = jnp.minimum(t1, float(win))
        diff = s / cnt - u[:, lo:hi]
        m_ref[:, lo:hi] = jnp.dot(diff.astype(BF16), w_ref[g], preferred_element_type=F32)

    o_ref[...] = x + _rms(m_ref[...], gpost_ref[...])


def _pool_layer(xp, w_scaled, gpost):
    tm = TM_POOL
    kern = functools.partial(_pool_kernel, tm=tm)
    consts = (w_scaled, gpost)
    return pl.pallas_call(
        kern,
        grid=(N_BLK, ROWS // tm),
        in_specs=[pl.BlockSpec(memory_space=pl.ANY)] + [_const_spec(c.shape) for c in consts],
        out_specs=_row_spec(tm),
        out_shape=INTERLEAVED,
        scratch_shapes=[
            pltpu.VMEM((POOL_RING, tm, D_MODEL), F32),
            pltpu.SemaphoreType.DMA((POOL_RING,)),
            pltpu.VMEM((POOL_HALO + tm, D_MODEL), F32),
            pltpu.VMEM((tm, D_MODEL), F32),
        ],
        compiler_params=pltpu.CompilerParams(
            dimension_semantics=("arbitrary", "arbitrary"),
            vmem_limit_bytes=VMEM_LIMIT),
        name="pool_mixer",
    )(xp, *consts)


def _ffn_kernel(x_ref, gpre_ref, wg_ref, wv_ref, cw_ref, cb_ref, wd_ref, gpost_ref, o_ref,
                hb_ref, carry_ref, g_ref, v_ref, hdn_ref, *, tm, out_seq_major):
    j = pl.program_id(1)
    halo = (CONV_WIDTH - 1) * SUBLANES

    @pl.when(j == 0)
    def _():
        carry_ref[...] = jnp.zeros_like(carry_ref)

    x = x_ref[...]
    hb_ref[...] = _rms(x, gpre_ref[...]).astype(BF16)

    for c in range(FF_NCHUNK):
        s = c % 2
        g_ref[s, 0:halo, :] = carry_ref[c]
        cols = slice(c * FF_CHUNK, (c + 1) * FF_CHUNK)
        g_ref[s, halo:halo + tm, :] = jnp.dot(hb_ref[...], wg_ref[:, cols], preferred_element_type=F32)
        v_ref[s] = jnp.dot(hb_ref[...], wv_ref[:, cols], preferred_element_type=F32)
        carry_ref[c] = g_ref[s, tm:tm + halo, :]
        cw = cw_ref[c]
        gc = cb_ref[c] + cw[0:1, :] * g_ref[s, 0:tm, :]
        gc = gc + cw[1:2, :] * g_ref[s, SUBLANES:tm + SUBLANES, :]
        gc = gc + cw[2:3, :] * g_ref[s, halo:halo + tm, :]
        inner = gc * (GELU_C1 + GELU_C2 * (gc * gc))
        hdn = (gc * v_ref[s]) * (1.0 + jnp.tanh(inner))
        hdn_ref[:, cols] = hdn.astype(BF16)

    f = jnp.dot(hdn_ref[...], wd_ref[...], preferred_element_type=F32)
    out = x + _rms(f, gpost_ref[...])
    if out_seq_major:
        for t in range(tm // SUBLANES):
            o_ref[:, t, :] = out[t * SUBLANES:(t + 1) * SUBLANES, :]
    else:
        o_ref[...] = out


def _ffn_layer(xp, gpre, wg, wv, cw, cb, wd_half, gpost, *, out_seq_major):
    tm = TM_FFN
    kern = functools.partial(_ffn_kernel, tm=tm, out_seq_major=out_seq_major)
    halo = (CONV_WIDTH - 1) * SUBLANES
    consts = (gpre, wg, wv, cw, cb, wd_half, gpost)
    return pl.pallas_call(
        kern,
        grid=(N_BLK, ROWS // tm),
        in_specs=[_row_spec(tm)] + [_const_spec(c.shape) for c in consts],
        out_specs=_seq_spec(tm) if out_seq_major else _row_spec(tm),
        out_shape=SEQ_MAJOR if out_seq_major else INTERLEAVED,
        scratch_shapes=[
            pltpu.VMEM((tm, D_MODEL), BF16),
            pltpu.VMEM((FF_NCHUNK, halo, FF_CHUNK), F32),
            pltpu.VMEM((2, halo + tm, FF_CHUNK), F32),
            pltpu.VMEM((2, tm, FF_CHUNK), F32),
            pltpu.VMEM((tm, D_FF), BF16),
        ],
        compiler_params=pltpu.CompilerParams(
            dimension_semantics=("arbitrary", "arbitrary"),
            vmem_limit_bytes=VMEM_LIMIT),
        name="conv_ffn",
    )(xp, *consts)


def _chunk_cols(w):
    k = w.shape[0]
    return jnp.transpose(w.reshape(k, FF_NCHUNK, FF_CHUNK), (1, 0, 2))


def kernel(x, s5_lambda_re, s5_lambda_im, s5_log_dt, s5_b_re, s5_b_im, s5_c_re, s5_c_im, s5_d, s5_w_glu, s5_b_glu, pool_w, pool_scale, ffn_w_gate, ffn_w_val, ffn_conv_w, ffn_conv_b, ffn_w_down, norm_mix_pre, norm_mix_post, norm_ffn_pre, norm_ffn_post):
    row = lambda v: v.reshape(1, D_MODEL)
    xp = x

    for i in range(DEPTH):
        jl = i // 2
        if i % 2 == 0:
            bin_, ab, wout = _s5_prep(s5_lambda_re[jl], s5_lambda_im[jl], s5_log_dt[jl],
                                      s5_b_re[jl], s5_b_im[jl], s5_c_re[jl], s5_c_im[jl])
            xp = _s5_layer(
                xp, row(norm_mix_pre[i]), bin_, ab, wout, row(s5_d[jl]),
                s5_w_glu[jl].astype(BF16), row(s5_b_glu[jl]), row(norm_mix_post[i]),
                x_seq_major=(i == 0))
        else:
            groups = (len(POOL_WINDOWS), POOL_GROUP_CH)
            w_scaled = (norm_mix_pre[i].reshape(groups)[:, :, None] * pool_w[jl]
                        * pool_scale[jl].reshape(groups)[:, None, :]).astype(BF16)
            xp = _pool_layer(xp, w_scaled, row(norm_mix_post[i]))
        xp = _ffn_layer(
            xp, row(norm_ffn_pre[i]),
            ffn_w_gate[i].astype(BF16), ffn_w_val[i].astype(BF16),
            _chunk_cols(ffn_conv_w[i]), ffn_conv_b[i].reshape(FF_NCHUNK, 1, FF_CHUNK),
            (0.5 * ffn_w_down[i]).astype(BF16),
            row(norm_ffn_post[i]), out_seq_major=(i == DEPTH - 1))

    return xp
```

```python
import functools
import math

import jax
import jax.numpy as jnp
from jax import lax
from jax.experimental import pallas as pl
from jax.experimental.pallas import tpu as pltpu

D_MODEL = 1024
BATCH = 32
SEQ = 2048
DEPTH = 4
S5_GROUP_CH = 16
S5_GROUPS = D_MODEL // S5_GROUP_CH
S5_STATE = 64
POOL_WINDOWS = (2, 4, 8, 16)
POOL_GROUP_CH = D_MODEL // len(POOL_WINDOWS)
D_FF = 2816
CONV_WIDTH = 3
RMS_EPS = 1e-6

SUBLANES = 8
LANES = 128

N_BLK = BATCH // SUBLANES
ROWS = SEQ * SUBLANES
S5_JBLK = LANES // S5_GROUP_CH
S5_NJ = S5_GROUPS // S5_JBLK
S5_JCH = S5_JBLK * S5_GROUP_CH
S5_JST = S5_JBLK * S5_STATE
S5_TB = 2
FF_CHUNK = 256
FF_NCHUNK = D_FF // FF_CHUNK
POOL_HALO = (max(POOL_WINDOWS) - 1) * SUBLANES + SUBLANES

TM_S5 = 1024
TM_POOL = 1024
TM_FFN = 1024
VMEM_LIMIT = 60 * 1024 * 1024

F32 = jnp.float32
BF16 = jnp.bfloat16

GELU_C1 = math.sqrt(2.0 / math.pi)
GELU_C2 = GELU_C1 * 0.044715


def _rms(x, gain):
    return x * lax.rsqrt(jnp.mean(x * x, axis=-1, keepdims=True) + RMS_EPS) * gain


def _cmul(ar, ai, br, bi):
    return ar * br - ai * bi, ar * bi + ai * br


def _s5_prep_kernel(lr_ref, li_ref, ldt_ref, br_ref, bi_ref, cr_ref, ci_ref,
                    ab_ref, bin_ref, wout_ref):
    lr = lr_ref[...]
    li = li_ref[...]
    dt = jnp.exp(ldt_ref[...])
    mag = jnp.exp(lr * dt)
    ab_re = mag * jnp.cos(li * dt)
    ab_im = mag * jnp.sin(li * dt)
    den = lr * lr + li * li
    nr = ab_re - 1.0
    ni = ab_im
    f_re = (nr * lr + ni * li) / den
    f_im = (ni * lr - nr * li) / den
    over_h = lambda v: v[:, None, :]
    bbr, bbi = _cmul(over_h(f_re), over_h(f_im), br_ref[...], bi_ref[...])

    pw = [None, (ab_re, ab_im)]
    for _ in range(S5_TB - 1):
        pw.append(_cmul(pw[-1][0], pw[-1][1], ab_re, ab_im))
    ab_ref[0] = pw[S5_TB][0]
    ab_ref[1] = pw[S5_TB][1]

    G, P, H = S5_GROUPS, S5_STATE, S5_GROUP_CH
    iota = lambda shape, dim: lax.broadcasted_iota(jnp.int32, shape, dim)
    flat = lambda v: v.reshape(G * H, P)

    spread_p = (iota((P, S5_JST), 1) % P == iota((P, S5_JST), 0)).astype(BF16)
    own_in = (iota((G * H, S5_JST), 0) // H) % S5_JBLK == iota((G * H, S5_JST), 1) // P

    for k in range(S5_TB):
        d = S5_TB - 1 - k
        piece = (bbr, bbi) if d == 0 else _cmul(over_h(pw[d][0]), over_h(pw[d][1]), bbr, bbi)
        for c in range(2):
            e = jnp.dot(flat(piece[c]).astype(BF16), spread_p, preferred_element_type=F32)
            e = jnp.where(own_in, e, 0.0).astype(BF16).reshape(S5_NJ, S5_JCH, S5_JST)
            bin_ref[:, k * S5_JCH:(k + 1) * S5_JCH, c * S5_JST:(c + 1) * S5_JST] = e

    cr = cr_ref[...]
    ci = ci_ref[...]
    ca = [(cr, ci)]
    for d in range(1, S5_TB + 1):
        ca.append(_cmul(over_h(pw[d][0]), over_h(pw[d][1]), cr, ci))

    nt = (((1,), (1,)), ((), ()))
    spread_rows = (iota((S5_JST, P), 0) % P == iota((S5_JST, P), 1)).astype(BF16)
    own_out = iota((S5_JST, S5_JCH), 0) // P == iota((S5_JST, S5_JCH), 1) // H
    own_t = iota((S5_JCH, S5_JCH), 0) // H == iota((S5_JCH, S5_JCH), 1) // H
    bb_flat = (flat(bbr), flat(bbi))
    ca_flat = [(flat(r), flat(i)) for r, i in ca]
    tch = S5_TB * S5_JCH
    for j in range(S5_NJ):
        rows = slice(j * S5_JCH, (j + 1) * S5_JCH)
        chunk = lambda k: slice(k * S5_JCH, (k + 1) * S5_JCH)
        hi = lax.Precision.HIGHEST
        for d in range(S5_TB):
            t = (lax.dot_general(bb_flat[0][rows], ca_flat[d][0][rows], nt, precision=hi, preferred_element_type=F32)
                 - lax.dot_general(bb_flat[1][rows], ca_flat[d][1][rows], nt, precision=hi, preferred_element_type=F32))
            t = jnp.where(own_t, t, 0.0).astype(BF16)
            for m in range(S5_TB - d):
                wout_ref[j, chunk(m), chunk(m + d)] = t
        for m in range(S5_TB):
            for k in range(m):
                wout_ref[j, chunk(m), chunk(k)] = jnp.zeros((S5_JCH, S5_JCH), BF16)
        for k in range(S5_TB):
            for c in range(2):
                e = lax.dot_general(spread_rows, ca_flat[k + 1][c][rows].astype(BF16), nt,
                                    preferred_element_type=F32)
                e = jnp.where(own_out, e if c == 0 else -e, 0.0).astype(BF16)
                wout_ref[j, tch + c * S5_JST:tch + (c + 1) * S5_JST, chunk(k)] = e


def _s5_prep(lam_re, lam_im, log_dt, b_re, b_im, c_re, c_im):
    G, P = S5_GROUPS, S5_STATE
    br_t = jnp.transpose(b_re, (0, 2, 1))
    bi_t = jnp.transpose(b_im, (0, 2, 1))
    tch = S5_TB * S5_JCH
    ab, bin_, wout = pl.pallas_call(
        _s5_prep_kernel,
        out_shape=(
            jax.ShapeDtypeStruct((2, G, P), F32),
            jax.ShapeDtypeStruct((S5_NJ, tch, 2 * S5_JST), BF16),
            jax.ShapeDtypeStruct((S5_NJ, tch + 2 * S5_JST, tch), BF16),
        ),
        name="s5_prep",
    )(lam_re, lam_im, log_dt.reshape(G, 1), br_t, bi_t, c_re, c_im)
    return bin_, ab.reshape(2, S5_NJ, 1, S5_JST), wout


def _s5_kernel(x_ref, gpre_ref, bin_ref, ab_ref, wout_ref, dskip_ref,
               wglu_ref, bglu_ref, gpost_ref, o_ref,
               x4_ref, u4_ref, ub4_ref, xs_ref, y4_ref, hst_ref, *in_ring, tm, x_seq_major):
    i = pl.program_id(0)
    j = pl.program_id(1)
    nj = pl.num_programs(1)
    m = tm // S5_TB
    nq = m // SUBLANES
    blk = S5_TB * SUBLANES

    if x_seq_major:
        xin, xsem = in_ring
        step = i * nj + j
        n_steps = pl.num_programs(0) * nj
        tsteps = tm // SUBLANES

        def row_copies(tile, slot):
            ti, tj = tile // nj, tile % nj
            return [pltpu.make_async_copy(x_ref.at[ti * SUBLANES + b, pl.ds(tj * tsteps, tsteps), :],
                                          xin.at[slot, :, b, :], xsem.at[slot]) for b in range(SUBLANES)]

        slot = lax.rem(step, 2)

        @pl.when(step == 0)
        def _():
            for cp in row_copies(0, 0):
                cp.start()

        @pl.when(step + 1 < n_steps)
        def _():
            for cp in row_copies(step + 1, 1 - slot):
                cp.start()

        for cp in row_copies(step, slot):
            cp.wait()

    @pl.when(j == 0)
    def _():
        hst_ref[...] = jnp.zeros_like(hst_ref)

    gpre = gpre_ref[...]
    def x_step(t):
        if x_seq_major:
            return xin[slot, t]
        return x_ref[t * SUBLANES:(t + 1) * SUBLANES, :]

    for k in range(S5_TB):
        xk = jnp.concatenate([x_step(q * S5_TB + k) for q in range(nq)], axis=0)
        x4_ref[k * m:(k + 1) * m, :] = xk
        uk = _rms(xk, gpre)
        u4_ref[k * m:(k + 1) * m, :] = uk
        ub4_ref[k * m:(k + 1) * m, :] = uk.astype(BF16)

    def block_inputs(jb):
        lo = jb * S5_JCH
        return jnp.concatenate([ub4_ref[k * m:(k + 1) * m, lo:lo + S5_JCH] for k in range(S5_TB)], axis=1)

    def in_proj(jb):
        xs_ref[jb % 2, SUBLANES:SUBLANES + m, :] = jnp.dot(block_inputs(jb), bin_ref[jb],
                                                             preferred_element_type=F32)

    in_proj(0)
    for jb in range(S5_NJ):
        s = jb % 2
        lo, hi_ = jb * S5_JCH, (jb + 1) * S5_JCH
        if jb + 1 < S5_NJ:
            in_proj(jb + 1)
        lhs = block_inputs(jb)
        ar, ai = (jnp.broadcast_to(ab_ref[n, jb], (SUBLANES, S5_JST)) for n in range(2))
        re, im = slice(0, S5_JST), slice(S5_JST, 2 * S5_JST)
        hr = hst_ref[jb, :, re]
        hi = hst_ref[jb, :, im]
        xs_ref[s, 0:SUBLANES, re] = hr
        xs_ref[s, 0:SUBLANES, im] = hi
        for q in range(nq):
            r0 = slice((q + 1) * SUBLANES, (q + 2) * SUBLANES)
            hr, hi = ar * hr - ai * hi + xs_ref[s, r0, re], ar * hi + ai * hr + xs_ref[s, r0, im]
            xs_ref[s, r0, re] = hr
            xs_ref[s, r0, im] = hi
        hst_ref[jb, :, re] = hr
        hst_ref[jb, :, im] = hi

        lhs_all = jnp.concatenate(
            [lhs, xs_ref[s, 0:m, re].astype(BF16), xs_ref[s, 0:m, im].astype(BF16)], axis=1)
        y4j = jnp.dot(lhs_all, wout_ref[jb], preferred_element_type=F32)
        for k in range(S5_TB):
            y4_ref[k * m:(k + 1) * m, lo:hi_] = y4j[:, k * S5_JCH:(k + 1) * S5_JCH]

    dskip = dskip_ref[...]
    bglu = bglu_ref[...]
    gpost = gpost_ref[...]
    for k in range(S5_TB):
        y = y4_ref[k * m:(k + 1) * m, :] + dskip * u4_ref[k * m:(k + 1) * m, :]
        y = jax.nn.gelu(y)
        z = jnp.dot(y.astype(BF16), wglu_ref[...], preferred_element_type=F32) + bglu
        out = x4_ref[k * m:(k + 1) * m, :] + _rms(y * jax.nn.sigmoid(z), gpost)
        for q in range(nq):
            r = q * blk + k * SUBLANES
            o_ref[r:r + SUBLANES, :] = out[q * SUBLANES:(q + 1) * SUBLANES, :]


def _const_spec(shape):
    nd = len(shape)
    return pl.BlockSpec(shape, lambda i, j: (0,) * nd, pipeline_mode=pl.Buffered(1))


def _row_spec(tm):
    return pl.BlockSpec((None, tm, D_MODEL), lambda i, j: (i, j, 0))


INTERLEAVED = jax.ShapeDtypeStruct((N_BLK, ROWS, D_MODEL), F32)
SEQ_MAJOR = jax.ShapeDtypeStruct((BATCH, SEQ, D_MODEL), F32)


def _s5_layer(xp, gpre, bin_, ab, wout, dskip, wglu, bglu, gpost, *, x_seq_major):
    tm = TM_S5
    m = tm // S5_TB
    kern = functools.partial(_s5_kernel, tm=tm, x_seq_major=x_seq_major)
    consts = (gpre, bin_, ab, wout, dskip, wglu, bglu, gpost)
    return pl.pallas_call(
        kern,
        grid=(N_BLK, ROWS // tm),
        in_specs=[pl.BlockSpec(memory_space=pl.ANY) if x_seq_major else _row_spec(tm)]
        + [_const_spec(c.shape) for c in consts],
        out_specs=_row_spec(tm),
        out_shape=INTERLEAVED,
        scratch_shapes=[
            pltpu.VMEM((tm, D_MODEL), F32),
            pltpu.VMEM((tm, D_MODEL), F32),
            pltpu.VMEM((tm, D_MODEL), BF16),
            pltpu.VMEM((2, SUBLANES + m, 2 * S5_JST), F32),
            pltpu.VMEM((tm, D_MODEL), F32),
            pltpu.VMEM((S5_NJ, SUBLANES, 2 * S5_JST), F32),
        ] + ([pltpu.VMEM((2, tm // SUBLANES, SUBLANES, D_MODEL), F32), pltpu.SemaphoreType.DMA((2,))]
             if x_seq_major else []),
        compiler_params=pltpu.CompilerParams(
            dimension_semantics=("arbitrary", "arbitrary"),
            vmem_limit_bytes=VMEM_LIMIT),
        name="s5_mixer",
    )(xp, *consts)


POOL_RING = 3


def _pool_kernel(x_hbm, w_ref, gpost_ref, o_ref, xring, sems, buf_ref, m_ref, *, tm):
    i = pl.program_id(0)
    j = pl.program_id(1)
    nj = pl.num_programs(1)
    step = i * nj + j
    n_steps = pl.num_programs(0) * nj

    def tile_copy(t):
        slot = lax.rem(t, POOL_RING)
        return pltpu.make_async_copy(x_hbm.at[t // nj, pl.ds((t % nj) * tm, tm), :], xring.at[slot], sems.at[slot])

    @pl.when(step == 0)
    def _():
        for t in range(POOL_RING - 1):
            tile_copy(t).start()

    @pl.when(step + (POOL_RING - 1) < n_steps)
    def _():
        tile_copy(step + (POOL_RING - 1)).start()

    tile_copy(step).wait()
    x_ref = xring.at[lax.rem(step, POOL_RING)]

    @pl.when(j == 0)
    def _():
        buf_ref[0:POOL_HALO, :] = jnp.zeros((POOL_HALO, D_MODEL), F32)

    @pl.when(j != 0)
    def _():
        buf_ref[0:POOL_HALO, :] = buf_ref[tm:tm + POOL_HALO, :]

    x = x_ref[...]
    u = x * lax.rsqrt(jnp.mean(x * x, axis=-1, keepdims=True) + RMS_EPS)
    buf_ref[POOL_HALO:POOL_HALO + tm, :] = u

    row = lax.broadcasted_iota(jnp.int32, (tm, 1), 0)
    t1 = (j * (tm // SUBLANES) + row // SUBLANES + 1).astype(F32)

    for g, win in enumerate(POOL_WINDOWS):
        lo, hi = g * POOL_GROUP_CH, (g + 1) * POOL_GROUP_CH
        s = buf_ref[:, lo:hi]
        span = 1
        while span < win:
            sh = span * SUBLANES
            s = s[sh:, :] + s[:-sh, :]
            span *= 2
        s = s[s.shape[0] - tm:, :]
        cnt = jnp.minimum(t1, float(win))
        diff = s / cnt - u[:, lo:hi]
        m_ref[:, lo:hi] = jnp.dot(diff.astype(BF16), w_ref[g], preferred_element_type=F32)

    o_ref[...] = x + _rms(m_ref[...], gpost_ref[...])


def _pool_layer(xp, w_scaled, gpost):
    tm = TM_POOL
    kern = functools.partial(_pool_kernel, tm=tm)
    consts = (w_scaled, gpost)
    return pl.pallas_call(
        kern,
        grid=(N_BLK, ROWS // tm),
        in_specs=[pl.BlockSpec(memory_space=pl.ANY)] + [_const_spec(c.shape) for c in consts],
        out_specs=_row_spec(tm),
        out_shape=INTERLEAVED,
        scratch_shapes=[
            pltpu.VMEM((POOL_RING, tm, D_MODEL), F32),
            pltpu.SemaphoreType.DMA((POOL_RING,)),
            pltpu.VMEM((POOL_HALO + tm, D_MODEL), F32),
            pltpu.VMEM((tm, D_MODEL), F32),
        ],
        compiler_params=pltpu.CompilerParams(
            dimension_semantics=("arbitrary", "arbitrary"),
            vmem_limit_bytes=VMEM_LIMIT),
        name="pool_mixer",
    )(xp, *consts)


def _ffn_kernel(x_ref, gpre_ref, wg_ref, wv_ref, cw_ref, cb_ref, wd_ref, gpost_ref, o_ref,
                hb_ref, carry_ref, g_ref, v_ref, hdn_ref, *out_ring, tm, out_seq_major):
    i = pl.program_id(0)
    j = pl.program_id(1)
    nj = pl.num_programs(1)
    halo = (CONV_WIDTH - 1) * SUBLANES

    if out_seq_major:
        obuf, osem = out_ring
        step = i * nj + j
        n_steps = pl.num_programs(0) * nj
        tsteps = tm // SUBLANES

        def row_copies(tile, slot):
            ti, tj = tile // nj, tile % nj
            return [pltpu.make_async_copy(obuf.at[slot, :, b, :],
                                          o_ref.at[ti * SUBLANES + b, pl.ds(tj * tsteps, tsteps), :],
                                          osem.at[slot]) for b in range(SUBLANES)]

        slot = lax.rem(step, 2)

        @pl.when(step >= 2)
        def _():
            for cp in row_copies(step - 2, slot):
                cp.wait()

    @pl.when(j == 0)
    def _():
        carry_ref[...] = jnp.zeros_like(carry_ref)

    x = x_ref[...]
    hb_ref[...] = _rms(x, gpre_ref[...]).astype(BF16)

    for c in range(FF_NCHUNK):
        s = c % 2
        g_ref[s, 0:halo, :] = carry_ref[c]
        cols = slice(c * FF_CHUNK, (c + 1) * FF_CHUNK)
        g_ref[s, halo:halo + tm, :] = jnp.dot(hb_ref[...], wg_ref[:, cols], preferred_element_type=F32)
        v_ref[s] = jnp.dot(hb_ref[...], wv_ref[:, cols], preferred_element_type=F32)
        carry_ref[c] = g_ref[s, tm:tm + halo, :]
        cw = cw_ref[c]
        gc = cb_ref[c] + cw[0:1, :] * g_ref[s, 0:tm, :]
        gc = gc + cw[1:2, :] * g_ref[s, SUBLANES:tm + SUBLANES, :]
        gc = gc + cw[2:3, :] * g_ref[s, halo:halo + tm, :]
        inner = gc * (GELU_C1 + GELU_C2 * (gc * gc))
        hdn = (gc * v_ref[s]) * (1.0 + jnp.tanh(inner))
        hdn_ref[:, cols] = hdn.astype(BF16)

    f = jnp.dot(hdn_ref[...], wd_ref[...], preferred_element_type=F32)
    out = x + _rms(f, gpost_ref[...])
    if out_seq_major:
        obuf[slot] = out.reshape(tsteps, SUBLANES, D_MODEL)
        for cp in row_copies(step, slot):
            cp.start()

        @pl.when(step == n_steps - 1)
        def _():
            for cp in row_copies(step - 1, 1 - slot) + row_copies(step, slot):
                cp.wait()
    else:
        o_ref[...] = out


def _ffn_layer(xp, gpre, wg, wv, cw, cb, wd_half, gpost, *, out_seq_major):
    tm = TM_FFN
    kern = functools.partial(_ffn_kernel, tm=tm, out_seq_major=out_seq_major)
    halo = (CONV_WIDTH - 1) * SUBLANES
    consts = (gpre, wg, wv, cw, cb, wd_half, gpost)
    return pl.pallas_call(
        kern,
        grid=(N_BLK, ROWS // tm),
        in_specs=[_row_spec(tm)] + [_const_spec(c.shape) for c in consts],
        out_specs=pl.BlockSpec(memory_space=pl.ANY) if out_seq_major else _row_spec(tm),
        out_shape=SEQ_MAJOR if out_seq_major else INTERLEAVED,
        scratch_shapes=[
            pltpu.VMEM((tm, D_MODEL), BF16),
            pltpu.VMEM((FF_NCHUNK, halo, FF_CHUNK), F32),
            pltpu.VMEM((2, halo + tm, FF_CHUNK), F32),
            pltpu.VMEM((2, tm, FF_CHUNK), F32),
            pltpu.VMEM((tm, D_FF), BF16),
        ] + ([pltpu.VMEM((2, tm // SUBLANES, SUBLANES, D_MODEL), F32), pltpu.SemaphoreType.DMA((2,))]
             if out_seq_major else []),
        compiler_params=pltpu.CompilerParams(
            dimension_semantics=("arbitrary", "arbitrary"),
            vmem_limit_bytes=VMEM_LIMIT),
        name="conv_ffn",
    )(xp, *consts)


def _chunk_cols(w):
    k = w.shape[0]
    return jnp.transpose(w.reshape(k, FF_NCHUNK, FF_CHUNK), (1, 0, 2))


def kernel(x, s5_lambda_re, s5_lambda_im, s5_log_dt, s5_b_re, s5_b_im, s5_c_re, s5_c_im, s5_d, s5_w_glu, s5_b_glu, pool_w, pool_scale, ffn_w_gate, ffn_w_val, ffn_conv_w, ffn_conv_b, ffn_w_down, norm_mix_pre, norm_mix_post, norm_ffn_pre, norm_ffn_post):
    row = lambda v: v.reshape(1, D_MODEL)
    xp = x

    for i in range(DEPTH):
        jl = i // 2
        if i % 2 == 0:
            bin_, ab, wout = _s5_prep(s5_lambda_re[jl], s5_lambda_im[jl], s5_log_dt[jl],
                                      s5_b_re[jl], s5_b_im[jl], s5_c_re[jl], s5_c_im[jl])
            xp = _s5_layer(
                xp, row(norm_mix_pre[i]), bin_, ab, wout, row(s5_d[jl]),
                s5_w_glu[jl].astype(BF16), row(s5_b_glu[jl]), row(norm_mix_post[i]),
                x_seq_major=(i == 0))
        else:
            groups = (len(POOL_WINDOWS), POOL_GROUP_CH)
            w_scaled = (norm_mix_pre[i].reshape(groups)[:, :, None] * pool_w[jl]
                        * pool_scale[jl].reshape(groups)[:, None, :]).astype(BF16)
            xp = _pool_layer(xp, w_scaled, row(norm_mix_post[i]))
        xp = _ffn_layer(
            xp, row(norm_ffn_pre[i]),
            ffn_w_gate[i].astype(BF16), ffn_w_val[i].astype(BF16),
            _chunk_cols(ffn_conv_w[i]), ffn_conv_b[i].reshape(FF_NCHUNK, 1, FF_CHUNK),
            (0.5 * ffn_w_down[i]).astype(BF16),
            row(norm_ffn_post[i]), out_seq_major=(i == DEPTH - 1))

    return xp
```

```python
import functools
import math

import jax
import jax.numpy as jnp
from jax import lax
from jax.experimental import pallas as pl
from jax.experimental.pallas import tpu as pltpu

D_MODEL = 1024
BATCH = 32
SEQ = 2048
DEPTH = 4
S5_GROUP_CH = 16
S5_GROUPS = D_MODEL // S5_GROUP_CH
S5_STATE = 64
POOL_WINDOWS = (2, 4, 8, 16)
POOL_GROUP_CH = D_MODEL // len(POOL_WINDOWS)
D_FF = 2816
CONV_WIDTH = 3
RMS_EPS = 1e-6

SUBLANES = 8
LANES = 128

N_BLK = BATCH // SUBLANES
ROWS = SEQ * SUBLANES
S5_JBLK = LANES // S5_GROUP_CH
S5_NJ = S5_GROUPS // S5_JBLK
S5_JCH = S5_JBLK * S5_GROUP_CH
S5_JST = S5_JBLK * S5_STATE
S5_TB = 2
FF_CHUNK = 256
FF_NCHUNK = D_FF // FF_CHUNK
POOL_HALO = (max(POOL_WINDOWS) - 1) * SUBLANES + SUBLANES

TM_S5 = 1024
TM_POOL = 1024
TM_FFN = 1024
VMEM_LIMIT = 60 * 1024 * 1024

F32 = jnp.float32
BF16 = jnp.bfloat16

GELU_C1 = math.sqrt(2.0 / math.pi)
GELU_C2 = GELU_C1 * 0.044715


def _rms(x, gain):
    return x * lax.rsqrt(jnp.mean(x * x, axis=-1, keepdims=True) + RMS_EPS) * gain


def _cmul(ar, ai, br, bi):
    return ar * br - ai * bi, ar * bi + ai * br


def _s5_prep_kernel(lr_ref, li_ref, ldt_ref, br_ref, bi_ref, cr_ref, ci_ref,
                    ab_ref, bin_ref, wout_ref):
    lr = lr_ref[...]
    li = li_ref[...]
    dt = jnp.exp(ldt_ref[...])
    mag = jnp.exp(lr * dt)
    ab_re = mag * jnp.cos(li * dt)
    ab_im = mag * jnp.sin(li * dt)
    den = lr * lr + li * li
    nr = ab_re - 1.0
    ni = ab_im
    f_re = (nr * lr + ni * li) / den
    f_im = (ni * lr - nr * li) / den
    over_h = lambda v: v[:, None, :]
    bbr, bbi = _cmul(over_h(f_re), over_h(f_im), br_ref[...], bi_ref[...])

    pw = [None, (ab_re, ab_im)]
    for _ in range(S5_TB - 1):
        pw.append(_cmul(pw[-1][0], pw[-1][1], ab_re, ab_im))
    ab_ref[0] = pw[S5_TB][0]
    ab_ref[1] = pw[S5_TB][1]

    G, P, H = S5_GROUPS, S5_STATE, S5_GROUP_CH
    iota = lambda shape, dim: lax.broadcasted_iota(jnp.int32, shape, dim)
    flat = lambda v: v.reshape(G * H, P)

    spread_p = (iota((P, S5_JST), 1) % P == iota((P, S5_JST), 0)).astype(BF16)
    own_in = (iota((G * H, S5_JST), 0) // H) % S5_JBLK == iota((G * H, S5_JST), 1) // P

    for k in range(S5_TB):
        d = S5_TB - 1 - k
        piece = (bbr, bbi) if d == 0 else _cmul(over_h(pw[d][0]), over_h(pw[d][1]), bbr, bbi)
        for c in range(2):
            e = jnp.dot(flat(piece[c]).astype(BF16), spread_p, preferred_element_type=F32)
            e = jnp.where(own_in, e, 0.0).astype(BF16).reshape(S5_NJ, S5_JCH, S5_JST)
            bin_ref[:, k * S5_JCH:(k + 1) * S5_JCH, c * S5_JST:(c + 1) * S5_JST] = e

    cr = cr_ref[...]
    ci = ci_ref[...]
    ca = [(cr, ci)]
    for d in range(1, S5_TB + 1):
        ca.append(_cmul(over_h(pw[d][0]), over_h(pw[d][1]), cr, ci))

    nt = (((1,), (1,)), ((), ()))
    spread_rows = (iota((S5_JST, P), 0) % P == iota((S5_JST, P), 1)).astype(BF16)
    own_out = iota((S5_JST, S5_JCH), 0) // P == iota((S5_JST, S5_JCH), 1) // H
    own_t = iota((S5_JCH, S5_JCH), 0) // H == iota((S5_JCH, S5_JCH), 1) // H
    bb_flat = (flat(bbr), flat(bbi))
    ca_flat = [(flat(r), flat(i)) for r, i in ca]
    tch = S5_TB * S5_JCH
    for j in range(S5_NJ):
        rows = slice(j * S5_JCH, (j + 1) * S5_JCH)
        chunk = lambda k: slice(k * S5_JCH, (k + 1) * S5_JCH)
        hi = lax.Precision.HIGHEST
        for d in range(S5_TB):
            t = (lax.dot_general(bb_flat[0][rows], ca_flat[d][0][rows], nt, precision=hi, preferred_element_type=F32)
                 - lax.dot_general(bb_flat[1][rows], ca_flat[d][1][rows], nt, precision=hi, preferred_element_type=F32))
            t = jnp.where(own_t, t, 0.0).astype(BF16)
            for m in range(S5_TB - d):
                wout_ref[j, chunk(m), chunk(m + d)] = t
        for m in range(S5_TB):
            for k in range(m):
                wout_ref[j, chunk(m), chunk(k)] = jnp.zeros((S5_JCH, S5_JCH), BF16)
        for k in range(S5_TB):
            for c in range(2):
                e = lax.dot_general(spread_rows, ca_flat[k + 1][c][rows].astype(BF16), nt,
                                    preferred_element_type=F32)
                e = jnp.where(own_out, e if c == 0 else -e, 0.0).astype(BF16)
                wout_ref[j, tch + c * S5_JST:tch + (c + 1) * S5_JST, chunk(k)] = e


def _s5_prep(lam_re, lam_im, log_dt, b_re, b_im, c_re, c_im):
    G, P = S5_GROUPS, S5_STATE
    br_t = jnp.transpose(b_re, (0, 2, 1))
    bi_t = jnp.transpose(b_im, (0, 2, 1))
    tch = S5_TB * S5_JCH
    ab, bin_, wout = pl.pallas_call(
        _s5_prep_kernel,
        out_shape=(
            jax.ShapeDtypeStruct((2, G, P), F32),
            jax.ShapeDtypeStruct((S5_NJ, tch, 2 * S5_JST), BF16),
            jax.ShapeDtypeStruct((S5_NJ, tch + 2 * S5_JST, tch), BF16),
        ),
        name="s5_prep",
    )(lam_re, lam_im, log_dt.reshape(G, 1), br_t, bi_t, c_re, c_im)
    return bin_, ab.reshape(2, S5_NJ, 1, S5_JST), wout


def _s5_kernel(x_ref, gpre_ref, bin_ref, ab_ref, wout_ref, dskip_ref,
               wglu_ref, bglu_ref, gpost_ref, o_ref,
               x4_ref, u4_ref, ub4_ref, xs_ref, y4_ref, hst_ref, *in_ring, tm, x_seq_major):
    i = pl.program_id(0)
    j = pl.program_id(1)
    nj = pl.num_programs(1)
    m = tm // S5_TB
    nq = m // SUBLANES
    blk = S5_TB * SUBLANES

    if x_seq_major:
        xin, xsem = in_ring
        step = i * nj + j
        n_steps = pl.num_programs(0) * nj
        tsteps = tm // SUBLANES

        def row_copies(tile, slot):
            ti, tj = tile // nj, tile % nj
            return [pltpu.make_async_copy(x_ref.at[ti * SUBLANES + b, pl.ds(tj * tsteps, tsteps), :],
                                          xin.at[slot, :, b, :], xsem.at[slot]) for b in range(SUBLANES)]

        slot = lax.rem(step, 2)

        @pl.when(step == 0)
        def _():
            for cp in row_copies(0, 0):
                cp.start()

        @pl.when(step + 1 < n_steps)
        def _():
            for cp in row_copies(step + 1, 1 - slot):
                cp.start()

        for cp in row_copies(step, slot):
            cp.wait()

    @pl.when(j == 0)
    def _():
        hst_ref[...] = jnp.zeros_like(hst_ref)

    gpre = gpre_ref[...]
    def x_step(t):
        if x_seq_major:
            return xin[slot, t]
        return x_ref[t * SUBLANES:(t + 1) * SUBLANES, :]

    for k in range(S5_TB):
        xk = jnp.concatenate([x_step(q * S5_TB + k) for q in range(nq)], axis=0)
        x4_ref[k * m:(k + 1) * m, :] = xk
        uk = _rms(xk, gpre)
        u4_ref[k * m:(k + 1) * m, :] = uk
        ub4_ref[k * m:(k + 1) * m, :] = uk.astype(BF16)

    def block_inputs(jb):
        lo = jb * S5_JCH
        return jnp.concatenate([ub4_ref[k * m:(k + 1) * m, lo:lo + S5_JCH] for k in range(S5_TB)], axis=1)

    def in_proj(jb):
        xs_ref[jb % 2, SUBLANES:SUBLANES + m, :] = jnp.dot(block_inputs(jb), bin_ref[jb],
                                                             preferred_element_type=F32)

    in_proj(0)
    for jb in range(S5_NJ):
        s = jb % 2
        lo, hi_ = jb * S5_JCH, (jb + 1) * S5_JCH
        if jb + 1 < S5_NJ:
            in_proj(jb + 1)
        lhs = block_inputs(jb)
        ar, ai = (jnp.broadcast_to(ab_ref[n, jb], (SUBLANES, S5_JST)) for n in range(2))
        re, im = slice(0, S5_JST), slice(S5_JST, 2 * S5_JST)
        hr = hst_ref[jb, :, re]
        hi = hst_ref[jb, :, im]
        xs_ref[s, 0:SUBLANES, re] = hr
        xs_ref[s, 0:SUBLANES, im] = hi
        for q in range(nq):
            r0 = slice((q + 1) * SUBLANES, (q + 2) * SUBLANES)
            hr, hi = ar * hr - ai * hi + xs_ref[s, r0, re], ar * hi + ai * hr + xs_ref[s, r0, im]
            xs_ref[s, r0, re] = hr
            xs_ref[s, r0, im] = hi
        hst_ref[jb, :, re] = hr
        hst_ref[jb, :, im] = hi

        lhs_all = jnp.concatenate(
            [lhs, xs_ref[s, 0:m, re].astype(BF16), xs_ref[s, 0:m, im].astype(BF16)], axis=1)
        y4j = jnp.dot(lhs_all, wout_ref[jb], preferred_element_type=F32)
        for k in range(S5_TB):
            y4_ref[k * m:(k + 1) * m, lo:hi_] = y4j[:, k * S5_JCH:(k + 1) * S5_JCH]

    dskip = dskip_ref[...]
    bglu = bglu_ref[...]
    gpost = gpost_ref[...]
    for k in range(S5_TB):
        y = y4_ref[k * m:(k + 1) * m, :] + dskip * u4_ref[k * m:(k + 1) * m, :]
        h = y * (1.0 + jnp.tanh(y * (GELU_C1 + GELU_C2 * (y * y))))
        half_z = jnp.dot(h.astype(BF16), wglu_ref[...], preferred_element_type=F32) + bglu
        gated = (0.25 * h) * (1.0 + jnp.tanh(half_z))
        out = x4_ref[k * m:(k + 1) * m, :] + _rms(gated, gpost)
        for q in range(nq):
            r = q * blk + k * SUBLANES
            o_ref[r:r + SUBLANES, :] = out[q * SUBLANES:(q + 1) * SUBLANES, :]


def _const_spec(shape):
    nd = len(shape)
    return pl.BlockSpec(shape, lambda i, j: (0,) * nd, pipeline_mode=pl.Buffered(1))


def _row_spec(tm):
    return pl.BlockSpec((None, tm, D_MODEL), lambda i, j: (i, j, 0))


INTERLEAVED = jax.ShapeDtypeStruct((N_BLK, ROWS, D_MODEL), F32)
SEQ_MAJOR = jax.ShapeDtypeStruct((BATCH, SEQ, D_MODEL), F32)


def _s5_layer(xp, gpre, bin_, ab, wout, dskip, wglu_quarter, bglu_half, gpost, *, x_seq_major):
    wglu, bglu = wglu_quarter, bglu_half
    tm = TM_S5
    m = tm // S5_TB
    kern = functools.partial(_s5_kernel, tm=tm, x_seq_major=x_seq_major)
    consts = (gpre, bin_, ab, wout, dskip, wglu, bglu, gpost)
    return pl.pallas_call(
        kern,
        grid=(N_BLK, ROWS // tm),
        in_specs=[pl.BlockSpec(memory_space=pl.ANY) if x_seq_major else _row_spec(tm)]
        + [_const_spec(c.shape) for c in consts],
        out_specs=_row_spec(tm),
        out_shape=INTERLEAVED,
        scratch_shapes=[
            pltpu.VMEM((tm, D_MODEL), F32),
            pltpu.VMEM((tm, D_MODEL), F32),
            pltpu.VMEM((tm, D_MODEL), BF16),
            pltpu.VMEM((2, SUBLANES + m, 2 * S5_JST), F32),
            pltpu.VMEM((tm, D_MODEL), F32),
            pltpu.VMEM((S5_NJ, SUBLANES, 2 * S5_JST), F32),
        ] + ([pltpu.VMEM((2, tm // SUBLANES, SUBLANES, D_MODEL), F32), pltpu.SemaphoreType.DMA((2,))]
             if x_seq_major else []),
        compiler_params=pltpu.CompilerParams(
            dimension_semantics=("arbitrary", "arbitrary"),
            vmem_limit_bytes=VMEM_LIMIT),
        name="s5_mixer",
    )(xp, *consts)


POOL_RING = 3


def _pool_kernel(x_hbm, w_ref, gpost_ref, o_ref, xring, sems, buf_ref, m_ref, *, tm):
    i = pl.program_id(0)
    j = pl.program_id(1)
    nj = pl.num_programs(1)
    step = i * nj + j
    n_steps = pl.num_programs(0) * nj

    def tile_copy(t):
        slot = lax.rem(t, POOL_RING)
        return pltpu.make_async_copy(x_hbm.at[t // nj, pl.ds((t % nj) * tm, tm), :], xring.at[slot], sems.at[slot])

    @pl.when(step == 0)
    def _():
        for t in range(POOL_RING - 1):
            tile_copy(t).start()

    @pl.when(step + (POOL_RING - 1) < n_steps)
    def _():
        tile_copy(step + (POOL_RING - 1)).start()

    tile_copy(step).wait()
    x_ref = xring.at[lax.rem(step, POOL_RING)]

    @pl.when(j == 0)
    def _():
        buf_ref[0:POOL_HALO, :] = jnp.zeros((POOL_HALO, D_MODEL), F32)

    @pl.when(j != 0)
    def _():
        buf_ref[0:POOL_HALO, :] = buf_ref[tm:tm + POOL_HALO, :]

    x = x_ref[...]
    u = x * lax.rsqrt(jnp.mean(x * x, axis=-1, keepdims=True) + RMS_EPS)
    buf_ref[POOL_HALO:POOL_HALO + tm, :] = u

    row = lax.broadcasted_iota(jnp.int32, (tm, 1), 0)
    t1 = (j * (tm // SUBLANES) + row // SUBLANES + 1).astype(F32)

    for g, win in enumerate(POOL_WINDOWS):
        lo, hi = g * POOL_GROUP_CH, (g + 1) * POOL_GROUP_CH
        s = buf_ref[:, lo:hi]
        span = 1
        while span < win:
            sh = span * SUBLANES
            s = s[sh:, :] + s[:-sh, :]
            span *= 2
        s = s[s.shape[0] - tm:, :]
        cnt = jnp.minimum(t1, float(win))
        diff = s / cnt - u[:, lo:hi]
        m_ref[:, lo:hi] = jnp.dot(diff.astype(BF16), w_ref[g], preferred_element_type=F32)

    o_ref[...] = x + _rms(m_ref[...], gpost_ref[...])


def _pool_layer(xp, w_scaled, gpost):
    tm = TM_POOL
    kern = functools.partial(_pool_kernel, tm=tm)
    consts = (w_scaled, gpost)
    return pl.pallas_call(
        kern,
        grid=(N_BLK, ROWS // tm),
        in_specs=[pl.BlockSpec(memory_space=pl.ANY)] + [_const_spec(c.shape) for c in consts],
        out_specs=_row_spec(tm),
        out_shape=INTERLEAVED,
        scratch_shapes=[
            pltpu.VMEM((POOL_RING, tm, D_MODEL), F32),
            pltpu.SemaphoreType.DMA((POOL_RING,)),
            pltpu.VMEM((POOL_HALO + tm, D_MODEL), F32),
            pltpu.VMEM((tm, D_MODEL), F32),
        ],
        compiler_params=pltpu.CompilerParams(
            dimension_semantics=("arbitrary", "arbitrary"),
            vmem_limit_bytes=VMEM_LIMIT),
        name="pool_mixer",
    )(xp, *consts)


def _ffn_kernel(x_ref, gpre_ref, wg_ref, wv_ref, cw_ref, cb_ref, wd_ref, gpost_ref, o_ref,
                hb_ref, carry_ref, g_ref, v_ref, hdn_ref, *out_ring, tm, out_seq_major):
    i = pl.program_id(0)
    j = pl.program_id(1)
    nj = pl.num_programs(1)
    halo = (CONV_WIDTH - 1) * SUBLANES

    if out_seq_major:
        obuf, osem = out_ring
        step = i * nj + j
        n_steps = pl.num_programs(0) * nj
        tsteps = tm // SUBLANES

        def row_copies(tile, slot):
            ti, tj = tile // nj, tile % nj
            return [pltpu.make_async_copy(obuf.at[slot, :, b, :],
                                          o_ref.at[ti * SUBLANES + b, pl.ds(tj * tsteps, tsteps), :],
                                          osem.at[slot]) for b in range(SUBLANES)]

        slot = lax.rem(step, 2)

        @pl.when(step >= 2)
        def _():
            for cp in row_copies(step - 2, slot):
                cp.wait()

    @pl.when(j == 0)
    def _():
        carry_ref[...] = jnp.zeros_like(carry_ref)

    x = x_ref[...]
    hb_ref[...] = _rms(x, gpre_ref[...]).astype(BF16)

    for c in range(FF_NCHUNK):
        s = c % 2
        g_ref[s, 0:halo, :] = carry_ref[c]
        cols = slice(c * FF_CHUNK, (c + 1) * FF_CHUNK)
        g_ref[s, halo:halo + tm, :] = jnp.dot(hb_ref[...], wg_ref[:, cols], preferred_element_type=F32)
        v_ref[s] = jnp.dot(hb_ref[...], wv_ref[:, cols], preferred_element_type=F32)
        carry_ref[c] = g_ref[s, tm:tm + halo, :]
        cw = cw_ref[c]
        gc = cb_ref[c] + cw[0:1, :] * g_ref[s, 0:tm, :]
        gc = gc + cw[1:2, :] * g_ref[s, SUBLANES:tm + SUBLANES, :]
        gc = gc + cw[2:3, :] * g_ref[s, halo:halo + tm, :]
        inner = gc * (GELU_C1 + GELU_C2 * (gc * gc))
        hdn = (gc * v_ref[s]) * (1.0 + jnp.tanh(inner))
        hdn_ref[:, cols] = hdn.astype(BF16)

    f = jnp.dot(hdn_ref[...], wd_ref[...], preferred_element_type=F32)
    out = x + _rms(f, gpost_ref[...])
    if out_seq_major:
        obuf[slot] = out.reshape(tsteps, SUBLANES, D_MODEL)
        for cp in row_copies(step, slot):
            cp.start()

        @pl.when(step == n_steps - 1)
        def _():
            for cp in row_copies(step - 1, 1 - slot) + row_copies(step, slot):
                cp.wait()
    else:
        o_ref[...] = out


def _ffn_layer(xp, gpre, wg, wv, cw, cb, wd_half, gpost, *, out_seq_major):
    tm = TM_FFN
    kern = functools.partial(_ffn_kernel, tm=tm, out_seq_major=out_seq_major)
    halo = (CONV_WIDTH - 1) * SUBLANES
    consts = (gpre, wg, wv, cw, cb, wd_half, gpost)
    return pl.pallas_call(
        kern,
        grid=(N_BLK, ROWS // tm),
        in_specs=[_row_spec(tm)] + [_const_spec(c.shape) for c in consts],
        out_specs=pl.BlockSpec(memory_space=pl.ANY) if out_seq_major else _row_spec(tm),
        out_shape=SEQ_MAJOR if out_seq_major else INTERLEAVED,
        scratch_shapes=[
            pltpu.VMEM((tm, D_MODEL), BF16),
            pltpu.VMEM((FF_NCHUNK, halo, FF_CHUNK), F32),
            pltpu.VMEM((2, halo + tm, FF_CHUNK), F32),
            pltpu.VMEM((2, tm, FF_CHUNK), F32),
            pltpu.VMEM((tm, D_FF), BF16),
        ] + ([pltpu.VMEM((2, tm // SUBLANES, SUBLANES, D_MODEL), F32), pltpu.SemaphoreType.DMA((2,))]
             if out_seq_major else []),
        compiler_params=pltpu.CompilerParams(
            dimension_semantics=("arbitrary", "arbitrary"),
            vmem_limit_bytes=VMEM_LIMIT),
        name="conv_ffn",
    )(xp, *consts)


def _chunk_cols(w):
    k = w.shape[0]
    return jnp.transpose(w.reshape(k, FF_NCHUNK, FF_CHUNK), (1, 0, 2))


def kernel(x, s5_lambda_re, s5_lambda_im, s5_log_dt, s5_b_re, s5_b_im, s5_c_re, s5_c_im, s5_d, s5_w_glu, s5_b_glu, pool_w, pool_scale, ffn_w_gate, ffn_w_val, ffn_conv_w, ffn_conv_b, ffn_w_down, norm_mix_pre, norm_mix_post, norm_ffn_pre, norm_ffn_post):
    row = lambda v: v.reshape(1, D_MODEL)
    xp = x

    for i in range(DEPTH):
        jl = i // 2
        if i % 2 == 0:
            bin_, ab, wout = _s5_prep(s5_lambda_re[jl], s5_lambda_im[jl], s5_log_dt[jl],
                                      s5_b_re[jl], s5_b_im[jl], s5_c_re[jl], s5_c_im[jl])
            xp = _s5_layer(
                xp, row(norm_mix_pre[i]), bin_, ab, wout, row(s5_d[jl]),
                (0.25 * s5_w_glu[jl]).astype(BF16), row(0.5 * s5_b_glu[jl]), row(norm_mix_post[i]),
                x_seq_major=(i == 0))
        else:
            groups = (len(POOL_WINDOWS), POOL_GROUP_CH)
            w_scaled = (norm_mix_pre[i].reshape(groups)[:, :, None] * pool_w[jl]
                        * pool_scale[jl].reshape(groups)[:, None, :]).astype(BF16)
            xp = _pool_layer(xp, w_scaled, row(norm_mix_post[i]))
        xp = _ffn_layer(
            xp, row(norm_ffn_pre[i]),
            ffn_w_gate[i].astype(BF16), ffn_w_val[i].astype(BF16),
            _chunk_cols(ffn_conv_w[i]), ffn_conv_b[i].reshape(FF_NCHUNK, 1, FF_CHUNK),
            (0.5 * ffn_w_down[i]).astype(BF16),
            row(norm_ffn_post[i]), out_seq_major=(i == DEPTH - 1))

    return xp
```

```python
import functools
import math

import jax
import jax.numpy as jnp
from jax import lax
from jax.experimental import pallas as pl
from jax.experimental.pallas import tpu as pltpu

D_MODEL = 1024
BATCH = 32
SEQ = 2048
DEPTH = 4
S5_GROUP_CH = 16
S5_GROUPS = D_MODEL // S5_GROUP_CH
S5_STATE = 64
POOL_WINDOWS = (2, 4, 8, 16)
POOL_GROUP_CH = D_MODEL // len(POOL_WINDOWS)
D_FF = 2816
CONV_WIDTH = 3
RMS_EPS = 1e-6

SUBLANES = 8
LANES = 128

N_BLK = BATCH // SUBLANES
ROWS = SEQ * SUBLANES
S5_JBLK = LANES // S5_GROUP_CH
S5_NJ = S5_GROUPS // S5_JBLK
S5_JCH = S5_JBLK * S5_GROUP_CH
S5_JST = S5_JBLK * S5_STATE
S5_TB = 2
FF_CHUNK = 256
FF_NCHUNK = D_FF // FF_CHUNK
POOL_HALO = (max(POOL_WINDOWS) - 1) * SUBLANES + SUBLANES

TM_S5 = 1024
TM_POOL = 1024
TM_FFN = 1024
VMEM_LIMIT = 60 * 1024 * 1024

F32 = jnp.float32
BF16 = jnp.bfloat16

GELU_C1 = math.sqrt(2.0 / math.pi)
GELU_C2 = GELU_C1 * 0.044715


def _rms(x, gain):
    return x * lax.rsqrt(jnp.mean(x * x, axis=-1, keepdims=True) + RMS_EPS) * gain


def _cmul(ar, ai, br, bi):
    return ar * br - ai * bi, ar * bi + ai * br


def _s5_prep_kernel(lr_ref, li_ref, ldt_ref, br_ref, bi_ref, cr_ref, ci_ref,
                    ab_ref, bin_ref, wout_ref):
    lr = lr_ref[...]
    li = li_ref[...]
    dt = jnp.exp(ldt_ref[...])
    mag = jnp.exp(lr * dt)
    ab_re = mag * jnp.cos(li * dt)
    ab_im = mag * jnp.sin(li * dt)
    den = lr * lr + li * li
    nr = ab_re - 1.0
    ni = ab_im
    f_re = (nr * lr + ni * li) / den
    f_im = (ni * lr - nr * li) / den
    over_h = lambda v: v[:, None, :]
    bbr, bbi = _cmul(over_h(f_re), over_h(f_im), br_ref[...], bi_ref[...])

    pw = [None, (ab_re, ab_im)]
    for _ in range(S5_TB - 1):
        pw.append(_cmul(pw[-1][0], pw[-1][1], ab_re, ab_im))
    ab_ref[0] = pw[S5_TB][0]
    ab_ref[1] = pw[S5_TB][1]

    G, P, H = S5_GROUPS, S5_STATE, S5_GROUP_CH
    iota = lambda shape, dim: lax.broadcasted_iota(jnp.int32, shape, dim)
    flat = lambda v: v.reshape(G * H, P)

    spread_p = (iota((P, S5_JST), 1) % P == iota((P, S5_JST), 0)).astype(BF16)
    own_in = (iota((G * H, S5_JST), 0) // H) % S5_JBLK == iota((G * H, S5_JST), 1) // P

    for k in range(S5_TB):
        d = S5_TB - 1 - k
        piece = (bbr, bbi) if d == 0 else _cmul(over_h(pw[d][0]), over_h(pw[d][1]), bbr, bbi)
        for c in range(2):
            e = jnp.dot(flat(piece[c]).astype(BF16), spread_p, preferred_element_type=F32)
            e = jnp.where(own_in, e, 0.0).astype(BF16).reshape(S5_NJ, S5_JCH, S5_JST)
            bin_ref[:, k * S5_JCH:(k + 1) * S5_JCH, c * S5_JST:(c + 1) * S5_JST] = e

    cr = cr_ref[...]
    ci = ci_ref[...]
    ca = [(cr, ci)]
    for d in range(1, S5_TB + 1):
        ca.append(_cmul(over_h(pw[d][0]), over_h(pw[d][1]), cr, ci))

    nt = (((1,), (1,)), ((), ()))
    spread_rows = (iota((S5_JST, P), 0) % P == iota((S5_JST, P), 1)).astype(BF16)
    own_out = iota((S5_JST, S5_JCH), 0) // P == iota((S5_JST, S5_JCH), 1) // H
    own_t = iota((S5_JCH, S5_JCH), 0) // H == iota((S5_JCH, S5_JCH), 1) // H
    bb_flat = (flat(bbr), flat(bbi))
    ca_flat = [(flat(r), flat(i)) for r, i in ca]
    tch = S5_TB * S5_JCH
    for j in range(S5_NJ):
        rows = slice(j * S5_JCH, (j + 1) * S5_JCH)
        chunk = lambda k: slice(k * S5_JCH, (k + 1) * S5_JCH)
        hi = lax.Precision.HIGHEST
        for d in range(S5_TB):
            t = (lax.dot_general(bb_flat[0][rows], ca_flat[d][0][rows], nt, precision=hi, preferred_element_type=F32)
                 - lax.dot_general(bb_flat[1][rows], ca_flat[d][1][rows], nt, precision=hi, preferred_element_type=F32))
            t = jnp.where(own_t, t, 0.0).astype(BF16)
            for m in range(S5_TB - d):
                wout_ref[j, chunk(m), chunk(m + d)] = t
        for m in range(S5_TB):
            for k in range(m):
                wout_ref[j, chunk(m), chunk(k)] = jnp.zeros((S5_JCH, S5_JCH), BF16)
        for k in range(S5_TB):
            for c in range(2):
                e = lax.dot_general(spread_rows, ca_flat[k + 1][c][rows].astype(BF16), nt,
                                    preferred_element_type=F32)
                e = jnp.where(own_out, e if c == 0 else -e, 0.0).astype(BF16)
                wout_ref[j, tch + c * S5_JST:tch + (c + 1) * S5_JST, chunk(k)] = e


def _s5_prep(lam_re, lam_im, log_dt, b_re, b_im, c_re, c_im):
    G, P = S5_GROUPS, S5_STATE
    br_t = jnp.transpose(b_re, (0, 2, 1))
    bi_t = jnp.transpose(b_im, (0, 2, 1))
    tch = S5_TB * S5_JCH
    ab, bin_, wout = pl.pallas_call(
        _s5_prep_kernel,
        out_shape=(
            jax.ShapeDtypeStruct((2, G, P), F32),
            jax.ShapeDtypeStruct((S5_NJ, tch, 2 * S5_JST), BF16),
            jax.ShapeDtypeStruct((S5_NJ, tch + 2 * S5_JST, tch), BF16),
        ),
        name="s5_prep",
    )(lam_re, lam_im, log_dt.reshape(G, 1), br_t, bi_t, c_re, c_im)
    return bin_, ab.reshape(2, S5_NJ, 1, S5_JST), wout


def _s5_kernel(x_ref, gpre_ref, bin_ref, ab_ref, wout_ref, dskip_ref,
               wglu_ref, bglu_ref, gpost_ref, o_ref,
               x4_ref, u4_ref, ub4_ref, xs_ref, y4_ref, hst_ref, *in_ring, tm, x_seq_major):
    i = pl.program_id(0)
    j = pl.program_id(1)
    nj = pl.num_programs(1)
    m = tm // S5_TB
    nq = m // SUBLANES
    blk = S5_TB * SUBLANES

    if x_seq_major:
        xin, xsem = in_ring
        step = i * nj + j
        n_steps = pl.num_programs(0) * nj
        tsteps = tm // SUBLANES

        def row_copies(tile, slot):
            ti, tj = tile // nj, tile % nj
            return [pltpu.make_async_copy(x_ref.at[ti * SUBLANES + b, pl.ds(tj * tsteps, tsteps), :],
                                          xin.at[slot, :, b, :], xsem.at[slot]) for b in range(SUBLANES)]

        slot = lax.rem(step, 2)

        @pl.when(step == 0)
        def _():
            for cp in row_copies(0, 0):
                cp.start()

        @pl.when(step + 1 < n_steps)
        def _():
            for cp in row_copies(step + 1, 1 - slot):
                cp.start()

        for cp in row_copies(step, slot):
            cp.wait()

    @pl.when(j == 0)
    def _():
        hst_ref[...] = jnp.zeros_like(hst_ref)

    gpre = gpre_ref[...]
    def x_step(t):
        if x_seq_major:
            return xin[slot, t]
        return x_ref[t * SUBLANES:(t + 1) * SUBLANES, :]

    for k in range(S5_TB):
        xk = jnp.concatenate([x_step(q * S5_TB + k) for q in range(nq)], axis=0)
        x4_ref[k * m:(k + 1) * m, :] = xk
        uk = _rms(xk, gpre)
        u4_ref[k * m:(k + 1) * m, :] = uk
        ub4_ref[k * m:(k + 1) * m, :] = uk.astype(BF16)

    def block_inputs(jb):
        lo = jb * S5_JCH
        return jnp.concatenate([ub4_ref[k * m:(k + 1) * m, lo:lo + S5_JCH] for k in range(S5_TB)], axis=1)

    def in_proj(jb):
        xs_ref[jb % 2, SUBLANES:SUBLANES + m, :] = jnp.dot(block_inputs(jb), bin_ref[jb],
                                                             preferred_element_type=F32)

    in_proj(0)
    for jb in range(S5_NJ):
        s = jb % 2
        lo, hi_ = jb * S5_JCH, (jb + 1) * S5_JCH
        if jb + 1 < S5_NJ:
            in_proj(jb + 1)
        lhs = block_inputs(jb)
        ar, ai = (jnp.broadcast_to(ab_ref[n, jb], (SUBLANES, S5_JST)) for n in range(2))
        re, im = slice(0, S5_JST), slice(S5_JST, 2 * S5_JST)
        hr = hst_ref[jb, :, re]
        hi = hst_ref[jb, :, im]
        xs_ref[s, 0:SUBLANES, re] = hr
        xs_ref[s, 0:SUBLANES, im] = hi
        for q in range(nq):
            r0 = slice((q + 1) * SUBLANES, (q + 2) * SUBLANES)
            hr, hi = ar * hr - ai * hi + xs_ref[s, r0, re], ar * hi + ai * hr + xs_ref[s, r0, im]
            xs_ref[s, r0, re] = hr
            xs_ref[s, r0, im] = hi
        hst_ref[jb, :, re] = hr
        hst_ref[jb, :, im] = hi

        lhs_all = jnp.concatenate(
            [lhs, xs_ref[s, 0:m, re].astype(BF16), xs_ref[s, 0:m, im].astype(BF16)], axis=1)
        y4j = jnp.dot(lhs_all, wout_ref[jb], preferred_element_type=F32)
        for k in range(S5_TB):
            y4_ref[k * m:(k + 1) * m, lo:hi_] = y4j[:, k * S5_JCH:(k + 1) * S5_JCH]

    dskip = dskip_ref[...]
    bglu = bglu_ref[...]
    gpost = gpost_ref[...]
    for k in range(S5_TB):
        y = y4_ref[k * m:(k + 1) * m, :] + dskip * u4_ref[k * m:(k + 1) * m, :]
        h = y * (1.0 + jnp.tanh(y * (GELU_C1 + GELU_C2 * (y * y))))
        half_z = jnp.dot(h.astype(BF16), wglu_ref[...], preferred_element_type=F32) + bglu
        gated = (0.25 * h) * (1.0 + jnp.tanh(half_z))
        out = x4_ref[k * m:(k + 1) * m, :] + _rms(gated, gpost)
        for q in range(nq):
            r = q * blk + k * SUBLANES
            o_ref[r:r + SUBLANES, :] = out[q * SUBLANES:(q + 1) * SUBLANES, :]


def _const_spec(shape):
    nd = len(shape)
    return pl.BlockSpec(shape, lambda i, j: (0,) * nd, pipeline_mode=pl.Buffered(1))


def _row_spec(tm):
    return pl.BlockSpec((None, tm, D_MODEL), lambda i, j: (i, j, 0))


INTERLEAVED = jax.ShapeDtypeStruct((N_BLK, ROWS, D_MODEL), F32)
SEQ_MAJOR = jax.ShapeDtypeStruct((BATCH, SEQ, D_MODEL), F32)


def _s5_layer(xp, gpre, bin_, ab, wout, dskip, wglu_quarter, bglu_half, gpost, *, x_seq_major):
    wglu, bglu = wglu_quarter, bglu_half
    tm = TM_S5
    m = tm // S5_TB
    kern = functools.partial(_s5_kernel, tm=tm, x_seq_major=x_seq_major)
    consts = (gpre, bin_, ab, wout, dskip, wglu, bglu, gpost)
    return pl.pallas_call(
        kern,
        grid=(N_BLK, ROWS // tm),
        in_specs=[pl.BlockSpec(memory_space=pl.ANY) if x_seq_major else _row_spec(tm)]
        + [_const_spec(c.shape) for c in consts],
        out_specs=_row_spec(tm),
        out_shape=INTERLEAVED,
        scratch_shapes=[
            pltpu.VMEM((tm, D_MODEL), F32),
            pltpu.VMEM((tm, D_MODEL), F32),
            pltpu.VMEM((tm, D_MODEL), BF16),
            pltpu.VMEM((2, SUBLANES + m, 2 * S5_JST), F32),
            pltpu.VMEM((tm, D_MODEL), F32),
            pltpu.VMEM((S5_NJ, SUBLANES, 2 * S5_JST), F32),
        ] + ([pltpu.VMEM((2, tm // SUBLANES, SUBLANES, D_MODEL), F32), pltpu.SemaphoreType.DMA((2,))]
             if x_seq_major else []),
        compiler_params=pltpu.CompilerParams(
            dimension_semantics=("arbitrary", "arbitrary"),
            vmem_limit_bytes=VMEM_LIMIT),
        name="s5_mixer",
    )(xp, *consts)


POOL_RING = 3


def _pool_kernel(x_hbm, w_ref, gpost_ref, o_ref, xring, sems, buf_ref, m_ref, *, tm):
    i = pl.program_id(0)
    j = pl.program_id(1)
    nj = pl.num_programs(1)
    step = i * nj + j
    n_steps = pl.num_programs(0) * nj

    def tile_copy(t):
        slot = lax.rem(t, POOL_RING)
        return pltpu.make_async_copy(x_hbm.at[t // nj, pl.ds((t % nj) * tm, tm), :], xring.at[slot], sems.at[slot])

    @pl.when(step == 0)
    def _():
        for t in range(POOL_RING - 1):
            tile_copy(t).start()

    @pl.when(step + (POOL_RING - 1) < n_steps)
    def _():
        tile_copy(step + (POOL_RING - 1)).start()

    tile_copy(step).wait()
    x_ref = xring.at[lax.rem(step, POOL_RING)]

    @pl.when(j == 0)
    def _():
        buf_ref[0:POOL_HALO, :] = jnp.zeros((POOL_HALO, D_MODEL), F32)

    @pl.when(j != 0)
    def _():
        buf_ref[0:POOL_HALO, :] = buf_ref[tm:tm + POOL_HALO, :]

    x = x_ref[...]
    u = x * lax.rsqrt(jnp.mean(x * x, axis=-1, keepdims=True) + RMS_EPS)
    buf_ref[POOL_HALO:POOL_HALO + tm, :] = u

    row = lax.broadcasted_iota(jnp.int32, (tm, 1), 0)
    t1 = (j * (tm // SUBLANES) + row // SUBLANES + 1).astype(F32)

    for g, win in enumerate(POOL_WINDOWS):
        lo, hi = g * POOL_GROUP_CH, (g + 1) * POOL_GROUP_CH
        s = buf_ref[:, lo:hi]
        span = 1
        while span < win:
            sh = span * SUBLANES
            s = s[sh:, :] + s[:-sh, :]
            span *= 2
        s = s[s.shape[0] - tm:, :]
        cnt = jnp.minimum(t1, float(win))
        diff = s / cnt - u[:, lo:hi]
        m_ref[:, lo:hi] = jnp.dot(diff.astype(BF16), w_ref[g], preferred_element_type=F32)

    o_ref[...] = x + _rms(m_ref[...], gpost_ref[...])


def _pool_layer(xp, w_scaled, gpost):
    tm = TM_POOL
    kern = functools.partial(_pool_kernel, tm=tm)
    consts = (w_scaled, gpost)
    return pl.pallas_call(
        kern,
        grid=(N_BLK, ROWS // tm),
        in_specs=[pl.BlockSpec(memory_space=pl.ANY)] + [_const_spec(c.shape) for c in consts],
        out_specs=_row_spec(tm),
        out_shape=INTERLEAVED,
        scratch_shapes=[
            pltpu.VMEM((POOL_RING, tm, D_MODEL), F32),
            pltpu.SemaphoreType.DMA((POOL_RING,)),
            pltpu.VMEM((POOL_HALO + tm, D_MODEL), F32),
            pltpu.VMEM((tm, D_MODEL), F32),
        ],
        compiler_params=pltpu.CompilerParams(
            dimension_semantics=("arbitrary", "arbitrary"),
            vmem_limit_bytes=VMEM_LIMIT),
        name="pool_mixer",
    )(xp, *consts)


def _ffn_kernel(x_ref, gpre_ref, wg_ref, wv_ref, cw_ref, cb_ref, wd_ref, gpost_ref, o_ref,
                hb_ref, carry_ref, g_ref, v_ref, hdn_ref, *out_ring, tm, out_seq_major):
    i = pl.program_id(0)
    j = pl.program_id(1)
    nj = pl.num_programs(1)
    halo = (CONV_WIDTH - 1) * SUBLANES

    if out_seq_major:
        obuf, osem = out_ring
        step = i * nj + j
        n_steps = pl.num_programs(0) * nj
        tsteps = tm // SUBLANES

        def row_copies(tile, slot):
            ti, tj = tile // nj, tile % nj
            return [pltpu.make_async_copy(obuf.at[slot, :, b, :],
                                          o_ref.at[ti * SUBLANES + b, pl.ds(tj * tsteps, tsteps), :],
                                          osem.at[slot]) for b in range(SUBLANES)]

        slot = lax.rem(step, 2)

        @pl.when(step >= 2)
        def _():
            for cp in row_copies(step - 2, slot):
                cp.wait()

    @pl.when(j == 0)
    def _():
        carry_ref[...] = jnp.zeros_like(carry_ref)

    x = x_ref[...]
    hb_ref[...] = _rms(x, gpre_ref[...]).astype(BF16)

    for c in range(FF_NCHUNK):
        s = c % 2
        g_ref[s, 0:halo, :] = carry_ref[c]
        cols = slice(c * FF_CHUNK, (c + 1) * FF_CHUNK)
        g_ref[s, halo:halo + tm, :] = jnp.dot(hb_ref[...], wg_ref[:, cols], preferred_element_type=F32)
        v_ref[s] = jnp.dot(hb_ref[...], wv_ref[:, cols], preferred_element_type=F32)
        carry_ref[c] = g_ref[s, tm:tm + halo, :]
        cw = cw_ref[c]
        gc = cb_ref[c] + cw[0:1, :] * g_ref[s, 0:tm, :]
        gc = gc + cw[1:2, :] * g_ref[s, SUBLANES:tm + SUBLANES, :]
        gc = gc + cw[2:3, :] * g_ref[s, halo:halo + tm, :]
        inner = gc * (GELU_C1 + GELU_C2 * (gc * gc))
        hdn = (gc * v_ref[s]) * (1.0 + jnp.tanh(inner))
        hdn_ref[:, cols] = hdn.astype(BF16)

    f = jnp.dot(hdn_ref[...], wd_ref[...], preferred_element_type=F32)
    out = x + _rms(f, gpost_ref[...])
    if out_seq_major:
        obuf[slot] = out.reshape(tsteps, SUBLANES, D_MODEL)
        for cp in row_copies(step, slot):
            cp.start()

        @pl.when(step == n_steps - 1)
        def _():
            for cp in row_copies(step - 1, 1 - slot) + row_copies(step, slot):
                cp.wait()
    else:
        o_ref[...] = out


def _ffn_layer(xp, gpre, wg_all, wv_all, cw, cb, wd_half, gpost, *, layer, out_seq_major):
    tm = TM_FFN
    kern = functools.partial(_ffn_kernel, tm=tm, out_seq_major=out_seq_major)
    halo = (CONV_WIDTH - 1) * SUBLANES
    consts = (gpre, wg_all, wv_all, cw, cb, wd_half, gpost)
    layer_spec = pl.BlockSpec((None, D_MODEL, D_FF), lambda i, j: (layer, 0, 0), pipeline_mode=pl.Buffered(1))
    const_specs = [_const_spec(c.shape) for c in consts]
    const_specs[1] = const_specs[2] = layer_spec
    return pl.pallas_call(
        kern,
        grid=(N_BLK, ROWS // tm),
        in_specs=[_row_spec(tm)] + const_specs,
        out_specs=pl.BlockSpec(memory_space=pl.ANY) if out_seq_major else _row_spec(tm),
        out_shape=SEQ_MAJOR if out_seq_major else INTERLEAVED,
        scratch_shapes=[
            pltpu.VMEM((tm, D_MODEL), BF16),
            pltpu.VMEM((FF_NCHUNK, halo, FF_CHUNK), F32),
            pltpu.VMEM((2, halo + tm, FF_CHUNK), F32),
            pltpu.VMEM((2, tm, FF_CHUNK), F32),
            pltpu.VMEM((tm, D_FF), BF16),
        ] + ([pltpu.VMEM((2, tm // SUBLANES, SUBLANES, D_MODEL), F32), pltpu.SemaphoreType.DMA((2,))]
             if out_seq_major else []),
        compiler_params=pltpu.CompilerParams(
            dimension_semantics=("arbitrary", "arbitrary"),
            vmem_limit_bytes=VMEM_LIMIT),
        name="conv_ffn",
    )(xp, *consts)


def _chunk_cols(w):
    k = w.shape[0]
    return jnp.transpose(w.reshape(k, FF_NCHUNK, FF_CHUNK), (1, 0, 2))


def kernel(x, s5_lambda_re, s5_lambda_im, s5_log_dt, s5_b_re, s5_b_im, s5_c_re, s5_c_im, s5_d, s5_w_glu, s5_b_glu, pool_w, pool_scale, ffn_w_gate, ffn_w_val, ffn_conv_w, ffn_conv_b, ffn_w_down, norm_mix_pre, norm_mix_post, norm_ffn_pre, norm_ffn_post):
    row = lambda v: v.reshape(1, D_MODEL)
    xp = x
    wg_all = ffn_w_gate.astype(BF16)
    wv_all = ffn_w_val.astype(BF16)

    for i in range(DEPTH):
        jl = i // 2
        if i % 2 == 0:
            bin_, ab, wout = _s5_prep(s5_lambda_re[jl], s5_lambda_im[jl], s5_log_dt[jl],
                                      s5_b_re[jl], s5_b_im[jl], s5_c_re[jl], s5_c_im[jl])
            xp = _s5_layer(
                xp, row(norm_mix_pre[i]), bin_, ab, wout, row(s5_d[jl]),
                (0.25 * s5_w_glu[jl]).astype(BF16), row(0.5 * s5_b_glu[jl]), row(norm_mix_post[i]),
                x_seq_major=(i == 0))
        else:
            groups = (len(POOL_WINDOWS), POOL_GROUP_CH)
            w_scaled = (norm_mix_pre[i].reshape(groups)[:, :, None] * pool_w[jl]
                        * pool_scale[jl].reshape(groups)[:, None, :]).astype(BF16)
            xp = _pool_layer(xp, w_scaled, row(norm_mix_post[i]))
        xp = _ffn_layer(
            xp, row(norm_ffn_pre[i]), wg_all, wv_all,
            _chunk_cols(ffn_conv_w[i]), ffn_conv_b[i].reshape(FF_NCHUNK, 1, FF_CHUNK),
            (0.5 * ffn_w_down[i]).astype(BF16),
            row(norm_ffn_post[i]), layer=i, out_seq_major=(i == DEPTH - 1))

    return xp
```

```python
import functools
import math

import jax
import jax.numpy as jnp
from jax import lax
from jax.experimental import pallas as pl
from jax.experimental.pallas import tpu as pltpu

D_MODEL = 1024
BATCH = 32
SEQ = 2048
DEPTH = 4
S5_GROUP_CH = 16
S5_GROUPS = D_MODEL // S5_GROUP_CH
S5_STATE = 64
POOL_WINDOWS = (2, 4, 8, 16)
POOL_GROUP_CH = D_MODEL // len(POOL_WINDOWS)
D_FF = 2816
CONV_WIDTH = 3
RMS_EPS = 1e-6

SUBLANES = 8
LANES = 128

N_BLK = BATCH // SUBLANES
ROWS = SEQ * SUBLANES
S5_JBLK = LANES // S5_GROUP_CH
S5_NJ = S5_GROUPS // S5_JBLK
S5_JCH = S5_JBLK * S5_GROUP_CH
S5_JST = S5_JBLK * S5_STATE
S5_TB = 2
FF_CHUNK = 256
FF_NCHUNK = D_FF // FF_CHUNK
POOL_HALO = (max(POOL_WINDOWS) - 1) * SUBLANES + SUBLANES

TM_S5 = 1024
TM_POOL = 1024
TM_FFN = 1024
VMEM_LIMIT = 60 * 1024 * 1024

F32 = jnp.float32
BF16 = jnp.bfloat16

GELU_C1 = math.sqrt(2.0 / math.pi)
GELU_C2 = GELU_C1 * 0.044715


def _rms(x, gain):
    return x * lax.rsqrt(jnp.mean(x * x, axis=-1, keepdims=True) + RMS_EPS) * gain


def _cmul(ar, ai, br, bi):
    return ar * br - ai * bi, ar * bi + ai * br


def _s5_prep_kernel(lr_ref, li_ref, ldt_ref, br_ref, bi_ref, cr_ref, ci_ref,
                    ab_ref, bin_ref, wout_ref):
    lr = lr_ref[...]
    li = li_ref[...]
    dt = jnp.exp(ldt_ref[...])
    mag = jnp.exp(lr * dt)
    ab_re = mag * jnp.cos(li * dt)
    ab_im = mag * jnp.sin(li * dt)
    den = lr * lr + li * li
    nr = ab_re - 1.0
    ni = ab_im
    f_re = (nr * lr + ni * li) / den
    f_im = (ni * lr - nr * li) / den
    over_h = lambda v: v[:, None, :]
    bbr, bbi = _cmul(over_h(f_re), over_h(f_im), br_ref[...], bi_ref[...])

    pw = [None, (ab_re, ab_im)]
    for _ in range(S5_TB - 1):
        pw.append(_cmul(pw[-1][0], pw[-1][1], ab_re, ab_im))
    ab_ref[0] = pw[S5_TB][0]
    ab_ref[1] = pw[S5_TB][1]

    G, P, H = S5_GROUPS, S5_STATE, S5_GROUP_CH
    iota = lambda shape, dim: lax.broadcasted_iota(jnp.int32, shape, dim)
    flat = lambda v: v.reshape(G * H, P)

    spread_p = (iota((P, S5_JST), 1) % P == iota((P, S5_JST), 0)).astype(BF16)
    own_in = (iota((G * H, S5_JST), 0) // H) % S5_JBLK == iota((G * H, S5_JST), 1) // P

    for k in range(S5_TB):
        d = S5_TB - 1 - k
        piece = (bbr, bbi) if d == 0 else _cmul(over_h(pw[d][0]), over_h(pw[d][1]), bbr, bbi)
        for c in range(2):
            e = jnp.dot(flat(piece[c]).astype(BF16), spread_p, preferred_element_type=F32)
            e = jnp.where(own_in, e, 0.0).astype(BF16).reshape(S5_NJ, S5_JCH, S5_JST)
            bin_ref[:, k * S5_JCH:(k + 1) * S5_JCH, c * S5_JST:(c + 1) * S5_JST] = e

    cr = cr_ref[...]
    ci = ci_ref[...]
    ca = [(cr, ci)]
    for d in range(1, S5_TB + 1):
        ca.append(_cmul(over_h(pw[d][0]), over_h(pw[d][1]), cr, ci))

    nt = (((1,), (1,)), ((), ()))
    spread_rows = (iota((S5_JST, P), 0) % P == iota((S5_JST, P), 1)).astype(BF16)
    own_out = iota((S5_JST, S5_JCH), 0) // P == iota((S5_JST, S5_JCH), 1) // H
    own_t = iota((S5_JCH, S5_JCH), 0) // H == iota((S5_JCH, S5_JCH), 1) // H
    bb_flat = (flat(bbr), flat(bbi))
    ca_flat = [(flat(r), flat(i)) for r, i in ca]
    tch = S5_TB * S5_JCH
    for j in range(S5_NJ):
        rows = slice(j * S5_JCH, (j + 1) * S5_JCH)
        chunk = lambda k: slice(k * S5_JCH, (k + 1) * S5_JCH)
        hi = lax.Precision.HIGHEST
        for d in range(S5_TB):
            t = (lax.dot_general(bb_flat[0][rows], ca_flat[d][0][rows], nt, precision=hi, preferred_element_type=F32)
                 - lax.dot_general(bb_flat[1][rows], ca_flat[d][1][rows], nt, precision=hi, preferred_element_type=F32))
            t = jnp.where(own_t, t, 0.0).astype(BF16)
            for m in range(S5_TB - d):
                wout_ref[j, chunk(m), chunk(m + d)] = t
        for m in range(S5_TB):
            for k in range(m):
                wout_ref[j, chunk(m), chunk(k)] = jnp.zeros((S5_JCH, S5_JCH), BF16)
        for k in range(S5_TB):
            for c in range(2):
                e = lax.dot_general(spread_rows, ca_flat[k + 1][c][rows].astype(BF16), nt,
                                    preferred_element_type=F32)
                e = jnp.where(own_out, e if c == 0 else -e, 0.0).astype(BF16)
                wout_ref[j, tch + c * S5_JST:tch + (c + 1) * S5_JST, chunk(k)] = e


def _s5_prep(lam_re, lam_im, log_dt, b_re, b_im, c_re, c_im):
    G, P = S5_GROUPS, S5_STATE
    br_t = jnp.transpose(b_re, (0, 2, 1))
    bi_t = jnp.transpose(b_im, (0, 2, 1))
    tch = S5_TB * S5_JCH
    ab, bin_, wout = pl.pallas_call(
        _s5_prep_kernel,
        out_shape=(
            jax.ShapeDtypeStruct((2, G, P), F32),
            jax.ShapeDtypeStruct((S5_NJ, tch, 2 * S5_JST), BF16),
            jax.ShapeDtypeStruct((S5_NJ, tch + 2 * S5_JST, tch), BF16),
        ),
        name="s5_prep",
    )(lam_re, lam_im, log_dt.reshape(G, 1), br_t, bi_t, c_re, c_im)
    return bin_, ab.reshape(2, S5_NJ, 1, S5_JST), wout


def _s5_kernel(x_ref, gpre_ref, bin_ref, ab_ref, wout_ref, dskip_ref,
               wglu_ref, bglu_ref, gpost_ref, o_ref,
               x4_ref, u4_ref, ub4_ref, xs_ref, y4_ref, hst_ref, *in_ring, tm, x_seq_major):
    i = pl.program_id(0)
    j = pl.program_id(1)
    nj = pl.num_programs(1)
    m = tm // S5_TB
    nq = m // SUBLANES
    blk = S5_TB * SUBLANES

    if x_seq_major:
        xin, xsem = in_ring
        step = i * nj + j
        n_steps = pl.num_programs(0) * nj
        tsteps = tm // SUBLANES

        def row_copies(tile, slot):
            ti, tj = tile // nj, tile % nj
            return [pltpu.make_async_copy(x_ref.at[ti * SUBLANES + b, pl.ds(tj * tsteps, tsteps), :],
                                          xin.at[slot, :, b, :], xsem.at[slot]) for b in range(SUBLANES)]

        slot = lax.rem(step, 2)

        @pl.when(step == 0)
        def _():
            for b, cp in enumerate(row_copies(0, 0)):
                cp.start(priority=b % 2)

        @pl.when(step + 1 < n_steps)
        def _():
            for b, cp in enumerate(row_copies(step + 1, 1 - slot)):
                cp.start(priority=b % 2)

        for cp in row_copies(step, slot):
            cp.wait()

    @pl.when(j == 0)
    def _():
        hst_ref[...] = jnp.zeros_like(hst_ref)

    gpre = gpre_ref[...]
    def x_step(t):
        if x_seq_major:
            return xin[slot, t]
        return x_ref[t * SUBLANES:(t + 1) * SUBLANES, :]

    for k in range(S5_TB):
        xk = jnp.concatenate([x_step(q * S5_TB + k) for q in range(nq)], axis=0)
        x4_ref[k * m:(k + 1) * m, :] = xk
        uk = _rms(xk, gpre)
        u4_ref[k * m:(k + 1) * m, :] = uk
        ub4_ref[k * m:(k + 1) * m, :] = uk.astype(BF16)

    def block_inputs(jb):
        lo = jb * S5_JCH
        return jnp.concatenate([ub4_ref[k * m:(k + 1) * m, lo:lo + S5_JCH] for k in range(S5_TB)], axis=1)

    def in_proj(jb):
        xs_ref[jb % 2, SUBLANES:SUBLANES + m, :] = jnp.dot(block_inputs(jb), bin_ref[jb],
                                                             preferred_element_type=F32)

    in_proj(0)
    for jb in range(S5_NJ):
        s = jb % 2
        lo, hi_ = jb * S5_JCH, (jb + 1) * S5_JCH
        if jb + 1 < S5_NJ:
            in_proj(jb + 1)
        lhs = block_inputs(jb)
        ar, ai = (jnp.broadcast_to(ab_ref[n, jb], (SUBLANES, S5_JST)) for n in range(2))
        re, im = slice(0, S5_JST), slice(S5_JST, 2 * S5_JST)
        hr = hst_ref[jb, :, re]
        hi = hst_ref[jb, :, im]
        xs_ref[s, 0:SUBLANES, re] = hr
        xs_ref[s, 0:SUBLANES, im] = hi
        for q in range(nq):
            r0 = slice((q + 1) * SUBLANES, (q + 2) * SUBLANES)
            hr, hi = ar * hr - ai * hi + xs_ref[s, r0, re], ar * hi + ai * hr + xs_ref[s, r0, im]
            xs_ref[s, r0, re] = hr
            xs_ref[s, r0, im] = hi
        hst_ref[jb, :, re] = hr
        hst_ref[jb, :, im] = hi

        lhs_all = jnp.concatenate(
            [lhs, xs_ref[s, 0:m, re].astype(BF16), xs_ref[s, 0:m, im].astype(BF16)], axis=1)
        y4j = jnp.dot(lhs_all, wout_ref[jb], preferred_element_type=F32)
        for k in range(S5_TB):
            y4_ref[k * m:(k + 1) * m, lo:hi_] = y4j[:, k * S5_JCH:(k + 1) * S5_JCH]

    dskip = dskip_ref[...]
    bglu = bglu_ref[...]
    gpost = gpost_ref[...]
    for k in range(S5_TB):
        y = y4_ref[k * m:(k + 1) * m, :] + dskip * u4_ref[k * m:(k + 1) * m, :]
        h = y * (1.0 + jnp.tanh(y * (GELU_C1 + GELU_C2 * (y * y))))
        half_z = jnp.dot(h.astype(BF16), wglu_ref[...], preferred_element_type=F32) + bglu
        gated = (0.25 * h) * (1.0 + jnp.tanh(half_z))
        out = x4_ref[k * m:(k + 1) * m, :] + _rms(gated, gpost)
        for q in range(nq):
            r = q * blk + k * SUBLANES
            o_ref[r:r + SUBLANES, :] = out[q * SUBLANES:(q + 1) * SUBLANES, :]


def _const_spec(shape):
    nd = len(shape)
    return pl.BlockSpec(shape, lambda i, j: (0,) * nd, pipeline_mode=pl.Buffered(1))


def _row_spec(tm):
    return pl.BlockSpec((None, tm, D_MODEL), lambda i, j: (i, j, 0))


INTERLEAVED = jax.ShapeDtypeStruct((N_BLK, ROWS, D_MODEL), F32)
SEQ_MAJOR = jax.ShapeDtypeStruct((BATCH, SEQ, D_MODEL), F32)


def _s5_layer(xp, gpre, bin_, ab, wout, dskip, wglu_quarter, bglu_half, gpost, *, x_seq_major):
    wglu, bglu = wglu_quarter, bglu_half
    tm = TM_S5
    m = tm // S5_TB
    kern = functools.partial(_s5_kernel, tm=tm, x_seq_major=x_seq_major)
    consts = (gpre, bin_, ab, wout, dskip, wglu, bglu, gpost)
    return pl.pallas_call(
        kern,
        grid=(N_BLK, ROWS // tm),
        in_specs=[pl.BlockSpec(memory_space=pl.ANY) if x_seq_major else _row_spec(tm)]
        + [_const_spec(c.shape) for c in consts],
        out_specs=_row_spec(tm),
        out_shape=INTERLEAVED,
        scratch_shapes=[
            pltpu.VMEM((tm, D_MODEL), F32),
            pltpu.VMEM((tm, D_MODEL), F32),
            pltpu.VMEM((tm, D_MODEL), BF16),
            pltpu.VMEM((2, SUBLANES + m, 2 * S5_JST), F32),
            pltpu.VMEM((tm, D_MODEL), F32),
            pltpu.VMEM((S5_NJ, SUBLANES, 2 * S5_JST), F32),
        ] + ([pltpu.VMEM((2, tm // SUBLANES, SUBLANES, D_MODEL), F32), pltpu.SemaphoreType.DMA((2,))]
             if x_seq_major else []),
        compiler_params=pltpu.CompilerParams(
            dimension_semantics=("arbitrary", "arbitrary"),
            vmem_limit_bytes=VMEM_LIMIT),
        name="s5_mixer",
    )(xp, *consts)


POOL_RING = 3


def _pool_kernel(x_hbm, w_ref, gpost_ref, o_ref, xring, sems, buf_ref, m_ref, *, tm):
    i = pl.program_id(0)
    j = pl.program_id(1)
    nj = pl.num_programs(1)
    step = i * nj + j
    n_steps = pl.num_programs(0) * nj

    def tile_copy(t):
        slot = lax.rem(t, POOL_RING)
        return pltpu.make_async_copy(x_hbm.at[t // nj, pl.ds((t % nj) * tm, tm), :], xring.at[slot], sems.at[slot])

    @pl.when(step == 0)
    def _():
        for t in range(POOL_RING - 1):
            tile_copy(t).start()

    @pl.when(step + (POOL_RING - 1) < n_steps)
    def _():
        tile_copy(step + (POOL_RING - 1)).start()

    tile_copy(step).wait()
    x_ref = xring.at[lax.rem(step, POOL_RING)]

    @pl.when(j == 0)
    def _():
        buf_ref[0:POOL_HALO, :] = jnp.zeros((POOL_HALO, D_MODEL), F32)

    @pl.when(j != 0)
    def _():
        buf_ref[0:POOL_HALO, :] = buf_ref[tm:tm + POOL_HALO, :]

    x = x_ref[...]
    u = x * lax.rsqrt(jnp.mean(x * x, axis=-1, keepdims=True) + RMS_EPS)
    buf_ref[POOL_HALO:POOL_HALO + tm, :] = u

    row = lax.broadcasted_iota(jnp.int32, (tm, 1), 0)
    t1 = (j * (tm // SUBLANES) + row // SUBLANES + 1).astype(F32)

    for g, win in enumerate(POOL_WINDOWS):
        lo, hi = g * POOL_GROUP_CH, (g + 1) * POOL_GROUP_CH
        s = buf_ref[:, lo:hi]
        span = 1
        while span < win:
            sh = span * SUBLANES
            s = s[sh:, :] + s[:-sh, :]
            span *= 2
        s = s[s.shape[0] - tm:, :]
        cnt = jnp.minimum(t1, float(win))
        diff = s / cnt - u[:, lo:hi]
        m_ref[:, lo:hi] = jnp.dot(diff.astype(BF16), w_ref[g], preferred_element_type=F32)

    o_ref[...] = x + _rms(m_ref[...], gpost_ref[...])


def _pool_layer(xp, w_scaled, gpost):
    tm = TM_POOL
    kern = functools.partial(_pool_kernel, tm=tm)
    consts = (w_scaled, gpost)
    return pl.pallas_call(
        kern,
        grid=(N_BLK, ROWS // tm),
        in_specs=[pl.BlockSpec(memory_space=pl.ANY)] + [_const_spec(c.shape) for c in consts],
        out_specs=_row_spec(tm),
        out_shape=INTERLEAVED,
        scratch_shapes=[
            pltpu.VMEM((POOL_RING, tm, D_MODEL), F32),
            pltpu.SemaphoreType.DMA((POOL_RING,)),
            pltpu.VMEM((POOL_HALO + tm, D_MODEL), F32),
            pltpu.VMEM((tm, D_MODEL), F32),
        ],
        compiler_params=pltpu.CompilerParams(
            dimension_semantics=("arbitrary", "arbitrary"),
            vmem_limit_bytes=VMEM_LIMIT),
        name="pool_mixer",
    )(xp, *consts)


def _ffn_kernel(x_ref, gpre_ref, wg_ref, wv_ref, cw_ref, cb_ref, wd_ref, gpost_ref, o_ref,
                hb_ref, carry_ref, g_ref, v_ref, hdn_ref, *out_ring, tm, out_seq_major):
    i = pl.program_id(0)
    j = pl.program_id(1)
    nj = pl.num_programs(1)
    halo = (CONV_WIDTH - 1) * SUBLANES

    if out_seq_major:
        obuf, osem = out_ring
        step = i * nj + j
        n_steps = pl.num_programs(0) * nj
        tsteps = tm // SUBLANES

        def row_copies(tile, slot):
            ti, tj = tile // nj, tile % nj
            return [pltpu.make_async_copy(obuf.at[slot, :, b, :],
                                          o_ref.at[ti * SUBLANES + b, pl.ds(tj * tsteps, tsteps), :],
                                          osem.at[slot]) for b in range(SUBLANES)]

        slot = lax.rem(step, 2)

        @pl.when(step >= 2)
        def _():
            for cp in row_copies(step - 2, slot):
                cp.wait()

    @pl.when(j == 0)
    def _():
        carry_ref[...] = jnp.zeros_like(carry_ref)

    x = x_ref[...]
    hb_ref[...] = _rms(x, gpre_ref[...]).astype(BF16)

    for c in range(FF_NCHUNK):
        s = c % 2
        g_ref[s, 0:halo, :] = carry_ref[c]
        cols = slice(c * FF_CHUNK, (c + 1) * FF_CHUNK)
        g_ref[s, halo:halo + tm, :] = jnp.dot(hb_ref[...], wg_ref[:, cols], preferred_element_type=F32)
        v_ref[s] = jnp.dot(hb_ref[...], wv_ref[:, cols], preferred_element_type=F32)
        carry_ref[c] = g_ref[s, tm:tm + halo, :]
        cw = cw_ref[c]
        gc = cb_ref[c] + cw[0:1, :] * g_ref[s, 0:tm, :]
        gc = gc + cw[1:2, :] * g_ref[s, SUBLANES:tm + SUBLANES, :]
        gc = gc + cw[2:3, :] * g_ref[s, halo:halo + tm, :]
        inner = gc * (GELU_C1 + GELU_C2 * (gc * gc))
        hdn = (gc * v_ref[s]) * (1.0 + jnp.tanh(inner))
        hdn_ref[:, cols] = hdn.astype(BF16)

    f = jnp.dot(hdn_ref[...], wd_ref[...], preferred_element_type=F32)
    out = x + _rms(f, gpost_ref[...])
    if out_seq_major:
        obuf[slot] = out.reshape(tsteps, SUBLANES, D_MODEL)
        for b, cp in enumerate(row_copies(step, slot)):
            cp.start(priority=b % 2)

        @pl.when(step == n_steps - 1)
        def _():
            for cp in row_copies(step - 1, 1 - slot) + row_copies(step, slot):
                cp.wait()
    else:
        o_ref[...] = out


def _ffn_layer(xp, gpre, wg_all, wv_all, cw, cb, wd_half, gpost, *, layer, out_seq_major):
    tm = TM_FFN
    kern = functools.partial(_ffn_kernel, tm=tm, out_seq_major=out_seq_major)
    halo = (CONV_WIDTH - 1) * SUBLANES
    consts = (gpre, wg_all, wv_all, cw, cb, wd_half, gpost)
    layer_spec = pl.BlockSpec((None, D_MODEL, D_FF), lambda i, j: (layer, 0, 0), pipeline_mode=pl.Buffered(1))
    const_specs = [_const_spec(c.shape) for c in consts]
    const_specs[1] = const_specs[2] = layer_spec
    return pl.pallas_call(
        kern,
        grid=(N_BLK, ROWS // tm),
        in_specs=[_row_spec(tm)] + const_specs,
        out_specs=pl.BlockSpec(memory_space=pl.ANY) if out_seq_major else _row_spec(tm),
        out_shape=SEQ_MAJOR if out_seq_major else INTERLEAVED,
        scratch_shapes=[
            pltpu.VMEM((tm, D_MODEL), BF16),
            pltpu.VMEM((FF_NCHUNK, halo, FF_CHUNK), F32),
            pltpu.VMEM((2, halo + tm, FF_CHUNK), F32),
            pltpu.VMEM((2, tm, FF_CHUNK), F32),
            pltpu.VMEM((tm, D_FF), BF16),
        ] + ([pltpu.VMEM((2, tm // SUBLANES, SUBLANES, D_MODEL), F32), pltpu.SemaphoreType.DMA((2,))]
             if out_seq_major else []),
        compiler_params=pltpu.CompilerParams(
            dimension_semantics=("arbitrary", "arbitrary"),
            vmem_limit_bytes=VMEM_LIMIT),
        name="conv_ffn",
    )(xp, *consts)


def _chunk_cols(w):
    k = w.shape[0]
    return jnp.transpose(w.reshape(k, FF_NCHUNK, FF_CHUNK), (1, 0, 2))


def kernel(x, s5_lambda_re, s5_lambda_im, s5_log_dt, s5_b_re, s5_b_im, s5_c_re, s5_c_im, s5_d, s5_w_glu, s5_b_glu, pool_w, pool_scale, ffn_w_gate, ffn_w_val, ffn_conv_w, ffn_conv_b, ffn_w_down, norm_mix_pre, norm_mix_post, norm_ffn_pre, norm_ffn_post):
    row = lambda v: v.reshape(1, D_MODEL)
    xp = x
    wg_all = ffn_w_gate.astype(BF16)
    wv_all = ffn_w_val.astype(BF16)

    for i in range(DEPTH):
        jl = i // 2
        if i % 2 == 0:
            bin_, ab, wout = _s5_prep(s5_lambda_re[jl], s5_lambda_im[jl], s5_log_dt[jl],
                                      s5_b_re[jl], s5_b_im[jl], s5_c_re[jl], s5_c_im[jl])
            xp = _s5_layer(
                xp, row(norm_mix_pre[i]), bin_, ab, wout, row(s5_d[jl]),
                (0.25 * s5_w_glu[jl]).astype(BF16), row(0.5 * s5_b_glu[jl]), row(norm_mix_post[i]),
                x_seq_major=(i == 0))
        else:
            groups = (len(POOL_WINDOWS), POOL_GROUP_CH)
            w_scaled = (norm_mix_pre[i].reshape(groups)[:, :, None] * pool_w[jl]
                        * pool_scale[jl].reshape(groups)[:, None, :]).astype(BF16)
            xp = _pool_layer(xp, w_scaled, row(norm_mix_post[i]))
        xp = _ffn_layer(
            xp, row(norm_ffn_pre[i]), wg_all, wv_all,
            _chunk_cols(ffn_conv_w[i]), ffn_conv_b[i].reshape(FF_NCHUNK, 1, FF_CHUNK),
            (0.5 * ffn_w_down[i]).astype(BF16),
            row(norm_ffn_post[i]), layer=i, out_seq_major=(i == DEPTH - 1))

    return xp
```
